```python
import math
import jax, jax.numpy as jnp
from jax import lax
import numpy as np

D_MODEL = 4096
BATCH = 1
SEQ = 16384
DEPTH = 2

D_MIX = D_MODEL
N_HEADS_MLA = D_MODEL // 256
QK_NOPE_DIM = 128
QK_ROPE_DIM = 64
V_HEAD_DIM = 128
Q_LORA_RANK = 768
KV_LORA_RANK = 512
ROPE_THETA = 10000.0
ATTN_BLOCK = 128
D_ATTN = N_HEADS_MLA * V_HEAD_DIM
D_SSM = D_MIX - D_ATTN
SSM_GROUP = 16
N_SSM_GROUPS = D_SSM // SSM_GROUP
SSM_STATE = 64
SSM_CHUNK = 128
DT_MIN = 1e-3
DT_MAX = 1e-1
OFF_CQ = Q_LORA_RANK
OFF_CKV = OFF_CQ + KV_LORA_RANK
OFF_KR = OFF_CKV + QK_ROPE_DIM
D_IN = OFF_KR + D_SSM
N_EXPERTS = 64
D_FF_EXPERT = 256
TOP_K = 8
N_EXPERT_GROUPS = 8
TOPK_GROUPS = 4
ROUTED_SCALE = 2.5
MOE_BLOCK = 128
DN_ALPHA = (2 * DEPTH) ** 0.25
DN_BETA = (8 * DEPTH) ** -0.25
LN_EPS = 1e-5
RMS_EPS = 1e-6

kernel_name = "hybrid_mla_s5_moe_deepnorm"


def _layernorm(x, g, b):
    xf = x.astype(jnp.float32)
    mu = jnp.mean(xf, -1, keepdims=True)
    var = jnp.mean(jnp.square(xf - mu), -1, keepdims=True)
    return ((xf - mu) * lax.rsqrt(var + LN_EPS) * g.astype(jnp.float32) + b.astype(jnp.float32)).astype(x.dtype)


def _rmsnorm(x, g):
    xf = x.astype(jnp.float32)
    return (xf * lax.rsqrt(jnp.mean(xf * xf, -1, keepdims=True) + RMS_EPS) * g.astype(jnp.float32)).astype(x.dtype)


def _rope(x, cos, sin):
    x1, x2 = jnp.split(x, 2, axis=-1)
    return jnp.concatenate([x1 * cos - x2 * sin, x1 * sin + x2 * cos], axis=-1)


def _mla(c_q, c_kv, k_rope, cos, sin, q_norm_g, kv_norm_g, w_uq, w_ukv):
    B, S, _ = c_q.shape
    H = N_HEADS_MLA
    q = (_rmsnorm(c_q, q_norm_g) @ w_uq).reshape(B, S, H, QK_NOPE_DIM + QK_ROPE_DIM)
    q_nope = q[..., :QK_NOPE_DIM]
    q_rope = _rope(q[..., QK_NOPE_DIM:], cos[:, :, None], sin[:, :, None])
    kv = (_rmsnorm(c_kv, kv_norm_g) @ w_ukv).reshape(B, S, H, QK_NOPE_DIM + V_HEAD_DIM)
    k_nope = kv[..., :QK_NOPE_DIM]
    v = kv[..., QK_NOPE_DIM:]
    k_r = _rope(k_rope, cos, sin)
    scale = (QK_NOPE_DIM + QK_ROPE_DIM) ** -0.5
    nb = S // ATTN_BLOCK
    qn_blk = q_nope.reshape(B, nb, ATTN_BLOCK, H, QK_NOPE_DIM).transpose(1, 0, 2, 3, 4)
    qr_blk = q_rope.reshape(B, nb, ATTN_BLOCK, H, QK_ROPE_DIM).transpose(1, 0, 2, 3, 4)
    kpos = jnp.arange(S)

    def block(args):
        i, qn, qr = args
        s = (jnp.einsum('bqhd,bkhd->bhqk', qn, k_nope, preferred_element_type=jnp.float32)
             + jnp.einsum('bqhr,bkr->bhqk', qr, k_r, preferred_element_type=jnp.float32))
        qpos = i * ATTN_BLOCK + jnp.arange(ATTN_BLOCK)
        s = jnp.where(kpos[None, :] <= qpos[:, None], s * scale, -jnp.inf)
        p = jax.nn.softmax(s, axis=-1)
        return jnp.einsum('bhqk,bkhd->bqhd', p.astype(v.dtype), v)

    o = lax.map(block, (jnp.arange(nb), qn_blk, qr_blk))
    return o.transpose(1, 0, 2, 3, 4).reshape(B, S, D_ATTN)


def _lin_op(e1, e2):
    a1, b1 = e1
    a2, b2 = e2
    return a1 * a2, a2 * b1 + b2


def _s5(u, lam_re, lam_im, log_dt, b_re, b_im, c_re, c_im, d_skip):
    B, S, _ = u.shape
    f32 = jnp.float32
    G, H, P = N_SSM_GROUPS, SSM_GROUP, SSM_STATE
    lam = lax.complex(jnp.minimum(lam_re.astype(f32), -1e-4), lam_im.astype(f32))
    dt = jnp.exp(log_dt.astype(f32))[:, None]
    lam_bar = jnp.exp(lam * dt)
    b = lax.complex(b_re.astype(f32), b_im.astype(f32))
    b_bar = ((lam_bar - 1.0) / lam)[..., None] * b
    c = lax.complex(c_re.astype(f32), c_im.astype(f32))
    d = d_skip.astype(f32)
    nc = S // SSM_CHUNK
    u_chunks = u.astype(f32).reshape(B, nc, SSM_CHUNK, G, H).transpose(1, 0, 2, 3, 4)

    def chunk_step(h, u_c):
        bu = jnp.einsum('gph,blgh->blgp', b_bar, u_c.astype(jnp.complex64))
        a = jnp.broadcast_to(lam_bar, bu.shape)
        a_cum, xs = lax.associative_scan(_lin_op, (a, bu), axis=1)
        xs = xs + a_cum * h[:, None]
        y = jnp.einsum('ghp,blgp->blgh', c, xs).real + d * u_c
        return xs[:, -1], y

    h0 = jnp.zeros((B, G, P), jnp.complex64)
    _, y = lax.scan(chunk_step, h0, u_chunks)
    return y.transpose(1, 0, 2, 3, 4).reshape(B, S, D_SSM).astype(u.dtype)


def _moe(x, w_router, router_bias, w_gate, w_up, w_down, ws_gate, ws_up, ws_down):
    B, S, D = x.shape
    T = B * S
    f32 = jnp.float32
    xt = x.reshape(T, D)
    scores = jax.nn.sigmoid(jnp.dot(xt, w_router, preferred_element_type=f32))
    sel = scores + router_bias.astype(f32)
    per_grp = N_EXPERTS // N_EXPERT_GROUPS
    grp_score = lax.top_k(sel.reshape(T, N_EXPERT_GROUPS, per_grp), 2)[0].sum(-1)
    _, top_g = lax.top_k(grp_score, TOPK_GROUPS)
    gmask = jax.nn.one_hot(top_g, N_EXPERT_GROUPS, dtype=f32).sum(1) > 0
    sel = jnp.where(jnp.repeat(gmask, per_grp, axis=1), sel, -jnp.inf)
    _, top_e = lax.top_k(sel, TOP_K)
    gate = jnp.take_along_axis(scores, top_e, axis=1)
    gate = gate / jnp.sum(gate, -1, keepdims=True) * ROUTED_SCALE

    n_assign = T * TOP_K
    n_blocks = n_assign // MOE_BLOCK + N_EXPERTS
    flat_e = top_e.reshape(-1).astype(jnp.int32)
    order = jnp.argsort(flat_e)
    sorted_e = flat_e[order]
    counts = jnp.bincount(flat_e, length=N_EXPERTS)
    starts = jnp.cumsum(counts) - counts
    padded = (counts + MOE_BLOCK - 1) // MOE_BLOCK * MOE_BLOCK
    pad_ends = jnp.cumsum(padded)
    pad_starts = pad_ends - padded
    dest = pad_starts[sorted_e] + jnp.arange(n_assign) - starts[sorted_e]
    rows = n_blocks * MOE_BLOCK
    tok_buf = jnp.full((rows,), T, jnp.int32).at[dest].set((order // TOP_K).astype(jnp.int32))
    gate_buf = jnp.zeros((rows,), f32).at[dest].set(gate.reshape(-1)[order])
    blk_expert = jnp.minimum(jnp.searchsorted(pad_ends, jnp.arange(n_blocks) * MOE_BLOCK, side='right'),
                             N_EXPERTS - 1)
    x_pad = jnp.concatenate([xt, jnp.zeros((1, D), xt.dtype)], axis=0)

    def expert_block(acc, args):
        e, tok, g = args
        xb = x_pad[tok]
        hb = jax.nn.silu(xb @ w_gate[e]) * (xb @ w_up[e])
        yb = (hb @ w_down[e]).astype(f32) * g[:, None]
        return acc.at[tok].add(yb), None

    acc, _ = lax.scan(expert_block, jnp.zeros((T + 1, D), f32),
                      (blk_expert, tok_buf.reshape(n_blocks, MOE_BLOCK), gate_buf.reshape(n_blocks, MOE_BLOCK)))
    shared = (jax.nn.silu(xt @ ws_gate) * (xt @ ws_up)) @ ws_down
    return (acc[:T] + shared.astype(f32)).astype(x.dtype).reshape(B, S, D)


def setup_inputs(seed: int = 0) -> dict:
    key = jax.random.key(seed)
    ks = jax.random.split(key, 32)
    f32 = jnp.float32

    def nrm(k, shape, s):
        return jax.random.normal(k, shape, f32) * s

    G, H, P = N_SSM_GROUPS, SSM_GROUP, SSM_STATE
    x = nrm(ks[0], (BATCH, SEQ, D_MODEL), 1.0)
    offset = jax.random.randint(ks[1], (BATCH, 1), 0, 4096, dtype=jnp.int32)
    positions = offset + jnp.arange(SEQ, dtype=jnp.int32)[None, :]
    return {
        "x": x,
        "positions": positions,
        "w_in": nrm(ks[2], (DEPTH, D_MODEL, D_IN), D_MODEL ** -0.5),
        "q_norm_g": 1.0 + nrm(ks[3], (DEPTH, Q_LORA_RANK), 0.02),
        "kv_norm_g": 1.0 + nrm(ks[4], (DEPTH, KV_LORA_RANK), 0.02),
        "w_uq": nrm(ks[5], (DEPTH, Q_LORA_RANK, N_HEADS_MLA * (QK_NOPE_DIM + QK_ROPE_DIM)), Q_LORA_RANK ** -0.5),
        "w_ukv": nrm(ks[6], (DEPTH, KV_LORA_RANK, N_HEADS_MLA * (QK_NOPE_DIM + V_HEAD_DIM)), KV_LORA_RANK ** -0.5),
        "ssm_lambda_re": -0.5 + nrm(ks[7], (DEPTH, G, P), 0.01),
        "ssm_lambda_im": jnp.pi * jnp.arange(P, dtype=f32)[None, None, :] + nrm(ks[8], (DEPTH, G, P), 0.01),
        "ssm_log_dt": jax.random.uniform(ks[9], (DEPTH, G), f32, math.log(DT_MIN), math.log(DT_MAX)),
        "ssm_b_re": nrm(ks[10], (DEPTH, G, P, H), (2 * H) ** -0.5),
        "ssm_b_im": nrm(ks[11], (DEPTH, G, P, H), (2 * H) ** -0.5),
        "ssm_c_re": nrm(ks[12], (DEPTH, G, H, P), (2 * P) ** -0.5),
        "ssm_c_im": nrm(ks[13], (DEPTH, G, H, P), (2 * P) ** -0.5),
        "ssm_d": nrm(ks[14], (DEPTH, G, H), 0.5),
        "w_glu": nrm(ks[15], (DEPTH, D_SSM, D_SSM), D_SSM ** -0.5),
        "b_glu": nrm(ks[16], (DEPTH, D_SSM), 0.01),
        "attn_out_norm_g": 1.0 + nrm(ks[17], (DEPTH, D_ATTN), 0.02),
        "ssm_out_norm_g": 1.0 + nrm(ks[18], (DEPTH, D_SSM), 0.02),
        "w_out": nrm(ks[19], (DEPTH, D_MIX, D_MODEL), D_MIX ** -0.5 * DN_BETA),
        "ln1_g": 1.0 + nrm(ks[20], (DEPTH, D_MODEL), 0.02),
        "ln1_b": nrm(ks[21], (DEPTH, D_MODEL), 0.01),
        "w_router": nrm(ks[22], (DEPTH, D_MODEL, N_EXPERTS), D_MODEL ** -0.5),
        "router_bias": nrm(ks[23], (DEPTH, N_EXPERTS), 0.01),
        "w_gate": nrm(ks[24], (DEPTH, N_EXPERTS, D_MODEL, D_FF_EXPERT), D_MODEL ** -0.5),
        "w_up": nrm(ks[25], (DEPTH, N_EXPERTS, D_MODEL, D_FF_EXPERT), D_MODEL ** -0.5),
        "w_down": nrm(ks[26], (DEPTH, N_EXPERTS, D_FF_EXPERT, D_MODEL), D_FF_EXPERT ** -0.5 * DN_BETA),
        "ws_gate": nrm(ks[27], (DEPTH, D_MODEL, D_FF_EXPERT), D_MODEL ** -0.5),
        "ws_up": nrm(ks[28], (DEPTH, D_MODEL, D_FF_EXPERT), D_MODEL ** -0.5),
        "ws_down": nrm(ks[29], (DEPTH, D_FF_EXPERT, D_MODEL), D_FF_EXPERT ** -0.5 * DN_BETA),
        "ln2_g": 1.0 + nrm(ks[30], (DEPTH, D_MODEL), 0.02),
        "ln2_b": nrm(ks[31], (DEPTH, D_MODEL), 0.01),
    }


def reference(x, positions, w_in, q_norm_g, kv_norm_g, w_uq, w_ukv, ssm_lambda_re, ssm_lambda_im,
              ssm_log_dt, ssm_b_re, ssm_b_im, ssm_c_re, ssm_c_im, ssm_d, w_glu, b_glu,
              attn_out_norm_g, ssm_out_norm_g, w_out, ln1_g, ln1_b, w_router, router_bias,
              w_gate, w_up, w_down, ws_gate, ws_up, ws_down, ln2_g, ln2_b):
    inv_freq = ROPE_THETA ** (-(jnp.arange(0, QK_ROPE_DIM, 2, dtype=jnp.float32) / QK_ROPE_DIM))
    ang = positions.astype(jnp.float32)[..., None] * inv_freq
    cos = jnp.cos(ang).astype(x.dtype)
    sin = jnp.sin(ang).astype(x.dtype)
    for l in range(DEPTH):
        proj = x @ w_in[l]
        c_q = proj[..., :OFF_CQ]
        c_kv = proj[..., OFF_CQ:OFF_CKV]
        k_rope = proj[..., OFF_CKV:OFF_KR]
        u = proj[..., OFF_KR:]
        attn = _mla(c_q, c_kv, k_rope, cos, sin, q_norm_g[l], kv_norm_g[l], w_uq[l], w_ukv[l])
        y = _s5(u, ssm_lambda_re[l], ssm_lambda_im[l], ssm_log_dt[l], ssm_b_re[l], ssm_b_im[l],
                ssm_c_re[l], ssm_c_im[l], ssm_d[l])
        g = jax.nn.gelu(y)
        ssm = g * jax.nn.sigmoid(g @ w_glu[l] + b_glu[l])
        heads = jnp.concatenate([_rmsnorm(attn, attn_out_norm_g[l]), _rmsnorm(ssm, ssm_out_norm_g[l])], axis=-1)
        x = _layernorm(DN_ALPHA * x + heads @ w_out[l], ln1_g[l], ln1_b[l])
        moe = _moe(x, w_router[l], router_bias[l], w_gate[l], w_up[l], w_down[l], ws_gate[l], ws_up[l], ws_down[l])
        x = _layernorm(DN_ALPHA * x + moe, ln2_g[l], ln2_b[l])
    return x
```

```python
import functools
import math

import jax
import jax.numpy as jnp
from jax import lax
from jax.experimental import pallas as pl
from jax.experimental.pallas import tpu as pltpu

F32 = jnp.float32
BF16 = jnp.bfloat16
I32 = jnp.int32
U32 = jnp.uint32

N_HEADS = 16
NOPE = 128
ROPE = 64
V_DIM = 128
HEAD_W = 256
Q_RANK = 768
KV_RANK = 512
ROPE_THETA = 10000.0
SSM_H = 16
SSM_P = 64
N_EXPERTS = 64
D_FF = 256
TOP_K = 8
N_GROUPS = 8
TOPK_GROUPS = 4
ROUTED_SCALE = 2.5
LN_EPS = 1e-5
RMS_EPS = 1e-6

LANES = 128
V7X_VMEM_LIMIT = 56 * 1024 * 1024
SSM_L = 16
SLAB_G = LANES // SSM_H
SLAB_STATE = SLAB_G * SSM_P
MOE_BM = 256


def _cparams(sem):
    return pltpu.CompilerParams(dimension_semantics=sem, vmem_limit_bytes=V7X_VMEM_LIMIT)


def _sigmoid(z):
    return 1.0 / (1.0 + jnp.exp(-z))


def _pack_bf16_pair(lo, hi):
    lo_bits = lax.bitcast_convert_type(lo.astype(BF16).astype(F32), U32)
    hi_bits = lax.bitcast_convert_type(hi.astype(BF16).astype(F32), U32)
    return (lo_bits >> 16) | (hi_bits & jnp.uint32(0xFFFF0000))


def _unpack_bf16_pair(w):
    lo = lax.bitcast_convert_type(w << 16, F32)
    hi = lax.bitcast_convert_type(w & jnp.uint32(0xFFFF0000), F32)
    return lo, hi


def _mm_kernel(a_ref, b_ref, o_ref):
    o_ref[...] = jnp.dot(a_ref[...], b_ref[...], preferred_element_type=F32).astype(o_ref.dtype)


def _matmul(a, b, out_dtype, tm, tn):
    m, k = a.shape
    n = b.shape[1]
    tm, tn = min(tm, m), min(tn, n)
    return pl.pallas_call(
        _mm_kernel,
        grid=(n // tn, m // tm),
        in_specs=[pl.BlockSpec((tm, k), lambda j, i: (i, 0)), pl.BlockSpec((k, tn), lambda j, i: (0, j))],
        out_specs=pl.BlockSpec((tm, tn), lambda j, i: (i, j)),
        out_shape=jax.ShapeDtypeStruct((m, n), out_dtype),
        compiler_params=_cparams(("arbitrary", "arbitrary")),
        name="proj_in",
    )(a, b)


def _rope_pair(t):
    r = t + pltpu.roll(t, 64, axis=1)
    lane = lax.broadcasted_iota(I32, r.shape, 1)
    return jnp.where(lane < ROPE, r, 0.0)


def _mla_prep_kernel(c_ref, cs_ref, qg_ref, kvg_ref, wq_ref, wkv_ref, q_ref, k_ref, v_ref, *, scale):
    cq = c_ref[:, 0:Q_RANK]
    ckv = c_ref[:, Q_RANK:Q_RANK + KV_RANK]
    kblk = c_ref[:, Q_RANK + KV_RANK:Q_RANK + KV_RANK + LANES]
    cs = cs_ref[...]
    cqn = (cq * lax.rsqrt(jnp.mean(cq * cq, -1, keepdims=True) + RMS_EPS) * qg_ref[...]).astype(BF16)
    ckvn = (ckv * lax.rsqrt(jnp.mean(ckv * ckv, -1, keepdims=True) + RMS_EPS) * kvg_ref[...]).astype(BF16)
    k_rope = _rope_pair(kblk * cs).astype(BF16)
    for h in range(N_HEADS):
        qh = jnp.dot(cqn, wq_ref[:, h * HEAD_W:(h + 1) * HEAD_W], preferred_element_type=F32)
        q_ref[:, h * HEAD_W:h * HEAD_W + NOPE] = (qh[:, :NOPE] * scale).astype(BF16)
        q_ref[:, h * HEAD_W + NOPE:(h + 1) * HEAD_W] = (_rope_pair(qh[:, NOPE:] * cs) * scale).astype(BF16)
        kvh = jnp.dot(ckvn, wkv_ref[:, h * HEAD_W:(h + 1) * HEAD_W], preferred_element_type=F32)
        k_ref[:, h * HEAD_W:h * HEAD_W + NOPE] = kvh[:, :NOPE].astype(BF16)
        k_ref[:, h * HEAD_W + NOPE:(h + 1) * HEAD_W] = k_rope
        v_ref[:, h * V_DIM:(h + 1) * V_DIM] = kvh[:, NOPE:].astype(BF16)


def _mla_prep(proj, cs, qg, kvg, wq, wkv, tm=256):
    s = proj.shape[0]
    tm = min(tm, s)
    cw = Q_RANK + KV_RANK + LANES
    scale = float((NOPE + ROPE) ** -0.5)
    return pl.pallas_call(
        functools.partial(_mla_prep_kernel, scale=scale),
        grid=(s // tm,),
        in_specs=[
            pl.BlockSpec((tm, cw), lambda i: (i, 0)),
            pl.BlockSpec((tm, LANES), lambda i: (i, 0)),
            pl.BlockSpec((1, Q_RANK), lambda i: (0, 0)),
            pl.BlockSpec((1, KV_RANK), lambda i: (0, 0)),
            pl.BlockSpec((Q_RANK, N_HEADS * HEAD_W), lambda i: (0, 0)),
            pl.BlockSpec((KV_RANK, N_HEADS * HEAD_W), lambda i: (0, 0)),
        ],
        out_specs=[
            pl.BlockSpec((tm, N_HEADS * HEAD_W), lambda i: (i, 0)),
            pl.BlockSpec((tm, N_HEADS * HEAD_W), lambda i: (i, 0)),
            pl.BlockSpec((tm, N_HEADS * V_DIM), lambda i: (i, 0)),
        ],
        out_shape=[
            jax.ShapeDtypeStruct((s, N_HEADS * HEAD_W), BF16),
            jax.ShapeDtypeStruct((s, N_HEADS * HEAD_W), BF16),
            jax.ShapeDtypeStruct((s, N_HEADS * V_DIM), BF16),
        ],
        compiler_params=_cparams(("arbitrary",)),
        name="mla_prep",
    )(proj, cs, qg, kvg, wq, wkv)


def _flash_kernel(q_ref, k_ref, v_ref, o_ref, m_scr, l_scr, acc_scr, *, tq, tk):
    i = pl.program_id(1)
    m_scr[...] = jnp.full(m_scr.shape, -jnp.inf, F32)
    l_scr[...] = jnp.zeros(l_scr.shape, F32)
    acc_scr[...] = jnp.zeros(acc_scr.shape, F32)
    q = q_ref[...]

    def block(j, masked):
        k = k_ref[pl.ds(pl.multiple_of(j * tk, tk), tk), :]
        v = v_ref[pl.ds(pl.multiple_of(j * tk, tk), tk), :]
        s = lax.dot_general(q, k, (((1,), (1,)), ((), ())), preferred_element_type=F32)
        if masked:
            qpos = i * tq + lax.broadcasted_iota(I32, s.shape, 0)
            kpos = j * tk + lax.broadcasted_iota(I32, s.shape, 1)
            s = jnp.where(kpos <= qpos, s, -jnp.inf)
        m_prev = m_scr[...]
        m_new = jnp.maximum(m_prev, jnp.max(s, -1, keepdims=True))
        alpha = jnp.exp(m_prev - m_new)
        p = jnp.exp(s - m_new)
        l_scr[...] = alpha * l_scr[...] + jnp.sum(p, -1, keepdims=True)
        acc_scr[...] = alpha * acc_scr[...] + jnp.dot(p.astype(BF16), v, preferred_element_type=F32)
        m_scr[...] = m_new

    n_full = i * (tq // tk)

    def full_body(j, carry):
        block(j, False)
        return carry

    lax.fori_loop(0, n_full, full_body, 0)
    for d in range(tq // tk):
        block(n_full + d, True)
    o_ref[...] = acc_scr[...] / l_scr[...]


def _flash(q, k, v, tq=512, tk=512):
    s = q.shape[0]
    tq, tk = min(tq, s), min(tk, s)
    return pl.pallas_call(
        functools.partial(_flash_kernel, tq=tq, tk=tk),
        grid=(N_HEADS, s // tq),
        in_specs=[
            pl.BlockSpec((tq, HEAD_W), lambda h, i: (i, h)),
            pl.BlockSpec((s, HEAD_W), lambda h, i: (0, h)),
            pl.BlockSpec((s, V_DIM), lambda h, i: (0, h)),
        ],
        out_specs=pl.BlockSpec((tq, V_DIM), lambda h, i: (i, h)),
        out_shape=jax.ShapeDtypeStruct((s, N_HEADS * V_DIM), F32),
        scratch_shapes=[pltpu.VMEM((tq, 1), F32), pltpu.VMEM((tq, 1), F32), pltpu.VMEM((tq, V_DIM), F32)],
        compiler_params=_cparams(("arbitrary", "arbitrary")),
        name="flash_attn",
    )(q, k, v)


def _s5_kernel(u_ref, kc_ref, win_ref, wout_ref, al_ref, y_ref, t_scr, ucat_scr, sc_scr, h_scr, *, nct):
    @pl.when(pl.program_id(1) == 0)
    def _():
        t_scr[...] = jnp.zeros(t_scr.shape, BF16)
        for s in range(SSM_L):
            for s2 in range(s, SSM_L):
                t_scr[s * LANES:(s + 1) * LANES, s2 * LANES:(s2 + 1) * LANES] = kc_ref[s2 - s]
        h_scr[...] = jnp.zeros(h_scr.shape, F32)

    for s in range(SSM_L):
        ucat_scr[:, s * LANES:(s + 1) * LANES] = u_ref[pl.ds(s, nct, stride=SSM_L), :].astype(BF16)
    ucat = ucat_scr[...]
    sc_scr[...] = jnp.dot(ucat, win_ref[...], preferred_element_type=F32)
    a_re = al_ref[:, :SLAB_STATE]
    a_im = al_ref[:, SLAB_STATE:]

    def step(c, carry):
        h_re, h_im = carry
        row = sc_scr[pl.ds(c, 1), :]
        sc_scr[pl.ds(c, 1), :] = jnp.concatenate([h_re, h_im], axis=1)
        n_re = a_re * h_re - a_im * h_im + row[:, :SLAB_STATE]
        n_im = a_re * h_im + a_im * h_re + row[:, SLAB_STATE:]
        return n_re, n_im

    h0 = h_scr[...]
    h_re, h_im = lax.fori_loop(0, nct, step, (h0[:, :SLAB_STATE], h0[:, SLAB_STATE:]))
    h_scr[...] = jnp.concatenate([h_re, h_im], axis=1)
    y = (jnp.dot(ucat, t_scr[...], preferred_element_type=F32)
         + jnp.dot(sc_scr[...].astype(BF16), wout_ref[...], preferred_element_type=F32))
    for s in range(SSM_L):
        y_ref[pl.ds(s, nct, stride=SSM_L), :] = y[:, s * LANES:(s + 1) * LANES]


def _s5(proj, u_col_block, kc, win, wout, al, ts=8192):
    s = proj.shape[0]
    ts = min(ts, s)
    nct = ts // SSM_L
    n_slab = kc.shape[0]
    lw = SSM_L * LANES
    return pl.pallas_call(
        functools.partial(_s5_kernel, nct=nct),
        grid=(n_slab, s // ts),
        in_specs=[
            pl.BlockSpec((ts, LANES), lambda j, t: (t, u_col_block + j)),
            pl.BlockSpec((None, SSM_L, LANES, LANES), lambda j, t: (j, 0, 0, 0)),
            pl.BlockSpec((None, lw, 2 * SLAB_STATE), lambda j, t: (j, 0, 0)),
            pl.BlockSpec((None, 2 * SLAB_STATE, lw), lambda j, t: (j, 0, 0)),
            pl.BlockSpec((None, 1, 2 * SLAB_STATE), lambda j, t: (j, 0, 0)),
        ],
        out_specs=pl.BlockSpec((ts, LANES), lambda j, t: (t, j)),
        out_shape=jax.ShapeDtypeStruct((s, n_slab * LANES), F32),
        scratch_shapes=[
            pltpu.VMEM((lw, lw), BF16),
            pltpu.VMEM((nct, lw), BF16),
            pltpu.VMEM((nct, 2 * SLAB_STATE), F32),
            pltpu.VMEM((1, 2 * SLAB_STATE), F32),
        ],
        compiler_params=_cparams(("arbitrary", "arbitrary")),
        name="s5",
    )(proj, kc, win, wout, al)


def _s5_params(lam_re, lam_im, log_dt, b_re, b_im, c_re, c_im, d_skip):
    g, p = lam_re.shape
    h = b_re.shape[-1]
    n_slab = g // SLAB_G
    hp = lax.Precision.HIGHEST
    lr = jnp.minimum(lam_re.astype(F32), -1e-4)
    li = lam_im.astype(F32)
    dt = jnp.exp(log_dt.astype(F32))[:, None]
    kk = jnp.arange(SSM_L + 1, dtype=F32)[:, None, None]
    mag = jnp.exp(lr * dt * kk)
    ang = li * dt * kk
    pw_re, pw_im = mag * jnp.cos(ang), mag * jnp.sin(ang)
    x, y = pw_re[1] - 1.0, pw_im[1]
    den = lr * lr + li * li
    f_re, f_im = (x * lr + y * li) / den, (y * lr - x * li) / den
    bb_re = f_re[..., None] * b_re - f_im[..., None] * b_im
    bb_im = f_re[..., None] * b_im + f_im[..., None] * b_re
    cp_re = c_re[None] * pw_re[:SSM_L, :, None, :] - c_im[None] * pw_im[:SSM_L, :, None, :]
    cp_im = c_re[None] * pw_im[:SSM_L, :, None, :] + c_im[None] * pw_re[:SSM_L, :, None, :]
    kmat = (jnp.einsum('kgop,gpi->gkoi', cp_re, bb_re, precision=hp)
            - jnp.einsum('kgop,gpi->gkoi', cp_im, bb_im, precision=hp))
    kmat = kmat.at[:, 0].add(jnp.eye(h, dtype=F32)[None] * d_skip.astype(F32)[:, :, None])
    eye = jnp.eye(SLAB_G, dtype=F32)
    kc = jnp.einsum('jakoi,ab->jkaibo', kmat.reshape(n_slab, SLAB_G, SSM_L, h, h), eye)
    kc = kc.reshape(n_slab, SSM_L, LANES, LANES).astype(BF16)
    rk = (SSM_L - 1) - jnp.arange(SSM_L, dtype=F32)[:, None, None]
    rmag, rang = jnp.exp(lr * dt * rk), li * dt * rk
    rp_re, rp_im = rmag * jnp.cos(rang), rmag * jnp.sin(rang)
    wi_re = rp_re[:, :, None, :] * bb_re.transpose(0, 2, 1)[None] - rp_im[:, :, None, :] * bb_im.transpose(0, 2, 1)[None]
    wi_im = rp_re[:, :, None, :] * bb_im.transpose(0, 2, 1)[None] + rp_im[:, :, None, :] * bb_re.transpose(0, 2, 1)[None]

    def slab_in(w):
        w = w.reshape(SSM_L, n_slab, SLAB_G, h, p)
        return jnp.einsum('sjaip,ab->jsaibp', w, eye).reshape(n_slab, SSM_L * LANES, SLAB_STATE)

    win = jnp.concatenate([slab_in(wi_re), slab_in(wi_im)], axis=-1).astype(BF16)
    q_re, q_im = pw_re[1:SSM_L + 1], pw_im[1:SSM_L + 1]
    cl_re = c_re[None] * q_re[:, :, None, :] - c_im[None] * q_im[:, :, None, :]
    cl_im = c_re[None] * q_im[:, :, None, :] + c_im[None] * q_re[:, :, None, :]

    def slab_out(w):
        w = w.reshape(SSM_L, n_slab, SLAB_G, h, p)
        return jnp.einsum('sjaop,ab->japsbo', w, eye).reshape(n_slab, SLAB_STATE, SSM_L * LANES)

    wout = jnp.concatenate([slab_out(cl_re), slab_out(-cl_im)], axis=1).astype(BF16)
    al = jnp.concatenate([pw_re[SSM_L].reshape(n_slab, 1, SLAB_STATE),
                          pw_im[SSM_L].reshape(n_slab, 1, SLAB_STATE)], axis=-1)
    return kc, win, wout, al


def _heads_kernel(attn_ref, y_ref, ag_ref, sg_ref, wglu_ref, bglu_ref, o_ref):
    a = attn_ref[...]
    da = a.shape[1]
    o_ref[:, :da] = (a * lax.rsqrt(jnp.mean(a * a, -1, keepdims=True) + RMS_EPS) * ag_ref[...]).astype(BF16)
    g = jax.nn.gelu(y_ref[...])
    z = jnp.dot(g.astype(BF16), wglu_ref[...], preferred_element_type=F32) + bglu_ref[...]
    ssm = g * _sigmoid(z)
    o_ref[:, da:] = (ssm * lax.rsqrt(jnp.mean(ssm * ssm, -1, keepdims=True) + RMS_EPS) * sg_ref[...]).astype(BF16)


def _heads(attn, y, ag, sg, wglu, bglu, tm=512):
    s, da = attn.shape
    ds = y.shape[1]
    tm = min(tm, s)
    return pl.pallas_call(
        _heads_kernel,
        grid=(s // tm,),
        in_specs=[
            pl.BlockSpec((tm, da), lambda i: (i, 0)),
            pl.BlockSpec((tm, ds), lambda i: (i, 0)),
            pl.BlockSpec((1, da), lambda i: (0, 0)),
            pl.BlockSpec((1, ds), lambda i: (0, 0)),
            pl.BlockSpec((ds, ds), lambda i: (0, 0)),
            pl.BlockSpec((1, ds), lambda i: (0, 0)),
        ],
        out_specs=pl.BlockSpec((tm, da + ds), lambda i: (i, 0)),
        out_shape=jax.ShapeDtypeStruct((s, da + ds), BF16),
        compiler_params=_cparams(("arbitrary",)),
        name="heads",
    )(attn, y, ag, sg, wglu, bglu)


def _layernorm_rows(x, g, b):
    mu = jnp.mean(x, -1, keepdims=True)
    xc = x - mu
    var = jnp.mean(xc * xc, -1, keepdims=True)
    return xc * lax.rsqrt(var + LN_EPS) * g + b


def _outproj_kernel(h_ref, w_ref, x_ref, g_ref, b_ref, o_ref, p_ref, *, alpha, tn, nj):
    j = pl.program_id(1)
    pre = alpha * x_ref[...] + jnp.dot(h_ref[...], w_ref[...], preferred_element_type=F32)
    for jj in range(nj):
        @pl.when(j == jj)
        def _(jj=jj):
            o_ref[:, jj * tn:(jj + 1) * tn] = pre

    @pl.when(j == nj - 1)
    def _():
        xn = _layernorm_rows(o_ref[...], g_ref[...], b_ref[...])
        o_ref[...] = xn
        half = xn.shape[1] // 2
        p_ref[...] = _pack_bf16_pair(xn[:, :half], xn[:, half:])


def _outproj_ln(heads, w, x, g, b, alpha, tm=512, tn=512):
    s, k = heads.shape
    n = w.shape[1]
    tm, tn = min(tm, s), min(tn, n)
    nj = n // tn
    return pl.pallas_call(
        functools.partial(_outproj_kernel, alpha=alpha, tn=tn, nj=nj),
        grid=(s // tm, nj),
        in_specs=[
            pl.BlockSpec((tm, k), lambda i, j: (i, 0)),
            pl.BlockSpec((k, tn), lambda i, j: (0, j)),
            pl.BlockSpec((tm, tn), lambda i, j: (i, j)),
            pl.BlockSpec((1, n), lambda i, j: (0, 0)),
            pl.BlockSpec((1, n), lambda i, j: (0, 0)),
        ],
        out_specs=[pl.BlockSpec((tm, n), lambda i, j: (i, 0)), pl.BlockSpec((tm, n // 2), lambda i, j: (i, 0))],
        out_shape=[jax.ShapeDtypeStruct((s, n), F32), jax.ShapeDtypeStruct((s, n // 2), U32)],
        compiler_params=_cparams(("arbitrary", "arbitrary")),
        name="outproj_ln",
    )(heads, w, x, g, b)


def _first_argmax(v, iota, n):
    m = jnp.max(v, axis=0, keepdims=True)
    first = jnp.min(jnp.where(v == m, iota, n), axis=0, keepdims=True)
    return m, first


def _router_kernel(x_ref, wh_ref, wl_ref, bias_ref, tri_ref, e_ref, g_ref, r_ref, cnt_ref, carry_scr):
    @pl.when(pl.program_id(0) == 0)
    def _():
        carry_scr[...] = jnp.zeros(carry_scr.shape, F32)

    x = x_ref[...]
    xh = x.astype(BF16)
    xl = (x - xh.astype(F32)).astype(BF16)
    dn = (((1,), (1,)), ((), ()))
    wh, wl = wh_ref[...], wl_ref[...]
    logits = (lax.dot_general(wh, xh, dn, preferred_element_type=F32)
              + lax.dot_general(wh, xl, dn, preferred_element_type=F32)
              + lax.dot_general(wl, xh, dn, preferred_element_type=F32))
    scores = _sigmoid(logits)
    sel = scores + bias_ref[...]
    t = sel.shape[1]
    per = N_EXPERTS // N_GROUPS
    ninf = -jnp.inf
    iota_g = lax.broadcasted_iota(I32, (per, t), 0)
    gs_rows = []
    for gi in range(N_GROUPS):
        blk = sel[gi * per:(gi + 1) * per, :]
        m1, f1 = _first_argmax(blk, iota_g, per)
        m2 = jnp.max(jnp.where(iota_g == f1, ninf, blk), axis=0, keepdims=True)
        gs_rows.append(m1 + m2)
    gs = jnp.concatenate(gs_rows, axis=0)
    iota_ng = lax.broadcasted_iota(I32, (N_GROUPS, t), 0)
    gmask = jnp.zeros((N_GROUPS, t), I32)
    for _ in range(TOPK_GROUPS):
        _, f = _first_argmax(gs, iota_ng, N_GROUPS)
        pick = iota_ng == f
        gmask = jnp.where(pick, 1, gmask)
        gs = jnp.where(pick, ninf, gs)
    selm = jnp.concatenate(
        [jnp.where(gmask[gi:gi + 1, :] > 0, sel[gi * per:(gi + 1) * per, :], ninf) for gi in range(N_GROUPS)], axis=0)
    iota_e = lax.broadcasted_iota(I32, (N_EXPERTS, t), 0)
    onehot = jnp.zeros((N_EXPERTS, t), F32)
    e_rows, g_rows = [], []
    for _ in range(TOP_K):
        _, f = _first_argmax(selm, iota_e, N_EXPERTS)
        pick = iota_e == f
        e_rows.append(f)
        g_rows.append(jnp.sum(jnp.where(pick, scores, 0.0), axis=0, keepdims=True))
        onehot = jnp.where(pick, 1.0, onehot)
        selm = jnp.where(pick, ninf, selm)
    gate = jnp.concatenate(g_rows, axis=0)
    gate = gate / jnp.sum(gate, axis=0, keepdims=True) * ROUTED_SCALE
    cum = jnp.dot(onehot.astype(BF16), tri_ref[...], preferred_element_type=F32)
    rank_e = cum - onehot + carry_scr[:, 0:1]
    r_rows = [jnp.sum(jnp.where(iota_e == f, rank_e, 0.0), axis=0, keepdims=True) for f in e_rows]
    carry = carry_scr[...] + cum[:, t - 1:t]
    carry_scr[...] = carry
    e_ref[...] = jnp.concatenate(e_rows, axis=0)
    g_ref[...] = gate
    r_ref[...] = jnp.concatenate(r_rows, axis=0).astype(I32)
    cnt_ref[...] = carry


def _router(x, wh, wl, bias, tm=512):
    t, d = x.shape
    tm = min(tm, t)
    tri = (jnp.arange(tm)[:, None] <= jnp.arange(tm)[None, :]).astype(BF16)
    return pl.pallas_call(
        _router_kernel,
        grid=(t // tm,),
        in_specs=[
            pl.BlockSpec((tm, d), lambda i: (i, 0)),
            pl.BlockSpec((N_EXPERTS, d), lambda i: (0, 0)),
            pl.BlockSpec((N_EXPERTS, d), lambda i: (0, 0)),
            pl.BlockSpec((N_EXPERTS, 1), lambda i: (0, 0)),
            pl.BlockSpec((tm, tm), lambda i: (0, 0)),
        ],
        out_specs=[
            pl.BlockSpec((TOP_K, tm), lambda i: (0, i)),
            pl.BlockSpec((TOP_K, tm), lambda i: (0, i)),
            pl.BlockSpec((TOP_K, tm), lambda i: (0, i)),
            pl.BlockSpec((N_EXPERTS, LANES), lambda i: (0, 0)),
        ],
        out_shape=[
            jax.ShapeDtypeStruct((TOP_K, t), I32),
            jax.ShapeDtypeStruct((TOP_K, t), F32),
            jax.ShapeDtypeStruct((TOP_K, t), I32),
            jax.ShapeDtypeStruct((N_EXPERTS, LANES), F32),
        ],
        scratch_shapes=[pltpu.VMEM((N_EXPERTS, LANES), F32)],
        compiler_params=_cparams(("arbitrary",)),
        name="router",
    )(x, wh, wl, bias, tri)


def _dispatch_kernel(pos_ref, x_hbm, xs_hbm, sem, *, tt):
    base = pl.program_id(0) * tt

    def issue(t, carry):
        for k in range(TOP_K):
            pltpu.make_async_copy(x_hbm.at[pl.ds(base + t, 1)], xs_hbm.at[pl.ds(pos_ref[k, t], 1)], sem).start()
        return carry

    lax.fori_loop(0, tt, issue, 0)
    for k in range(TOP_K):
        pltpu.make_async_copy(x_hbm.at[pl.ds(0, tt)], xs_hbm.at[pl.ds(0, tt)], sem).wait()


def _dispatch(pos, xp, tt=512):
    t, w = xp.shape
    tt = min(tt, t)
    return pl.pallas_call(
        functools.partial(_dispatch_kernel, tt=tt),
        grid=(t // tt,),
        in_specs=[
            pl.BlockSpec((TOP_K, tt), lambda i: (0, i), memory_space=pltpu.SMEM),
            pl.BlockSpec(memory_space=pl.ANY),
        ],
        out_specs=pl.BlockSpec(memory_space=pl.ANY),
        out_shape=jax.ShapeDtypeStruct((t * TOP_K, w), U32),
        scratch_shapes=[pltpu.SemaphoreType.DMA(())],
        compiler_params=_cparams(("arbitrary",)),
        name="dispatch",
    )(pos, xp)


def _experts_kernel(vblk_ref, vexp_ref, vlo_ref, vhi_ref, vfirst_ref, xs_ref, wgu_ref, wd_ref, ys_ref, *, bm):
    v = pl.program_id(0)
    lo, hi = vlo_ref[v], vhi_ref[v]

    @pl.when(hi > lo)
    def _():
        xa, xb = _unpack_bf16_pair(xs_ref[...])
        half = xa.shape[1]
        gu = (jnp.dot(xa.astype(BF16), wgu_ref[:half, :], preferred_element_type=F32)
              + jnp.dot(xb.astype(BF16), wgu_ref[half:, :], preferred_element_type=F32))
        hmid = (gu[:, :D_FF] * _sigmoid(gu[:, :D_FF])) * gu[:, D_FF:]
        y = jnp.dot(hmid.astype(BF16), wd_ref[...], preferred_element_type=F32)
        packed = _pack_bf16_pair(y[:, :half], y[:, half:])
        rows = vblk_ref[v] * bm + lax.broadcasted_iota(I32, packed.shape, 0)
        mine = (rows >= lo) & (rows < hi)

        @pl.when(vfirst_ref[v] == 1)
        def _():
            ys_ref[...] = jnp.where(mine, packed, jnp.uint32(0))

        @pl.when(vfirst_ref[v] == 0)
        def _():
            ys_ref[...] = jnp.where(mine, packed, ys_ref[...])


def _experts(sched, xs, wgu, wd, bm):
    r, w = xs.shape
    d = wd.shape[2]
    nv = sched[0].shape[0]
    grid_spec = pltpu.PrefetchScalarGridSpec(
        num_scalar_prefetch=5,
        grid=(nv,),
        in_specs=[
            pl.BlockSpec((bm, w), lambda v, vb, ve, lo, hi, fi: (vb[v], 0)),
            pl.BlockSpec((None, d, 2 * D_FF), lambda v, vb, ve, lo, hi, fi: (ve[v], 0, 0)),
            pl.BlockSpec((None, D_FF, d), lambda v, vb, ve, lo, hi, fi: (ve[v], 0, 0)),
        ],
        out_specs=pl.BlockSpec((bm, w), lambda v, vb, ve, lo, hi, fi: (vb[v], 0)),
    )
    return pl.pallas_call(
        functools.partial(_experts_kernel, bm=bm),
        grid_spec=grid_spec,
        out_shape=jax.ShapeDtypeStruct((r, w), U32),
        compiler_params=_cparams(("arbitrary",)),
        name="experts",
    )(*sched, xs, wgu, wd)


def _expert_schedule(counts, n_rows, bm):
    counts = counts.astype(I32)
    ends = jnp.cumsum(counts)
    starts = ends - counts
    nb = n_rows // bm
    nv = nb + N_EXPERTS - 1
    first_blk = starts // bm
    last_blk = jnp.maximum(ends - 1, 0) // bm
    nvis = jnp.where(counts > 0, last_blk - first_blk + 1, 0)
    vis_end = jnp.cumsum(nvis)
    vis_start = vis_end - nvis
    total = vis_end[-1]
    v = jnp.arange(nv, dtype=I32)
    e = jnp.minimum(jnp.searchsorted(vis_end, v, side='right'), N_EXPERTS - 1).astype(I32)
    blk = first_blk[e] + (v - vis_start[e])
    real = v < total
    last_v = jnp.maximum(total - 1, 0)
    e = jnp.where(real, e, e[last_v])
    blk = jnp.where(real, blk, blk[last_v]).astype(I32)
    lo = jnp.where(real, starts[e], 0).astype(I32)
    hi = jnp.where(real, ends[e], 0).astype(I32)
    prev_blk = jnp.concatenate([jnp.full((1,), -1, I32), blk[:-1]])
    first = (real & (blk != prev_blk)).astype(I32)
    return starts, (blk, e, lo, hi, first)


def _combine_kernel(pos_ref, ys_hbm, gate_ref, x_ref, wsgu_ref, wsd_ref, g_ref, b_ref, o_ref, ob_ref,
                    buf, sem, *, tc, alpha):
    def issue(t, carry):
        for k in range(TOP_K):
            pltpu.make_async_copy(ys_hbm.at[pl.ds(pos_ref[k, t], 1)], buf.at[k, pl.ds(t, 1)], sem).start()
        return carry

    lax.fori_loop(0, tc, issue, 0)
    x = x_ref[...]
    xb = x.astype(BF16)
    gu = jnp.dot(xb, wsgu_ref[...], preferred_element_type=F32)
    hmid = (gu[:, :D_FF] * _sigmoid(gu[:, :D_FF])) * gu[:, D_FF:]
    shared = jnp.dot(hmid.astype(BF16), wsd_ref[...], preferred_element_type=F32)
    for k in range(TOP_K):
        pltpu.make_async_copy(ys_hbm.at[pl.ds(0, tc)], buf.at[k], sem).wait()
    half = x.shape[1] // 2
    acc_lo = shared[:, :half]
    acc_hi = shared[:, half:]
    gate = gate_ref[...]
    for k in range(TOP_K):
        lo, hi = _unpack_bf16_pair(buf[k])
        gk = gate[:, k:k + 1]
        acc_lo = acc_lo + gk * lo
        acc_hi = acc_hi + gk * hi
    pre = alpha * x + jnp.concatenate([acc_lo, acc_hi], axis=1)
    xn = _layernorm_rows(pre, g_ref[...], b_ref[...])
    o_ref[...] = xn
    ob_ref[...] = xn.astype(BF16)


def _combine_ln(pos, ys, gate_tok, x, wsgu, wsd, g, b, alpha, tc=256):
    t, d = x.shape
    tc = min(tc, t)
    w = ys.shape[1]
    return pl.pallas_call(
        functools.partial(_combine_kernel, tc=tc, alpha=alpha),
        grid=(t // tc,),
        in_specs=[
            pl.BlockSpec((TOP_K, tc), lambda i: (0, i), memory_space=pltpu.SMEM),
            pl.BlockSpec(memory_space=pl.ANY),
            pl.BlockSpec((tc, TOP_K), lambda i: (i, 0)),
            pl.BlockSpec((tc, d), lambda i: (i, 0)),
            pl.BlockSpec((d, 2 * D_FF), lambda i: (0, 0)),
            pl.BlockSpec((D_FF, d), lambda i: (0, 0)),
            pl.BlockSpec((1, d), lambda i: (0, 0)),
            pl.BlockSpec((1, d), lambda i: (0, 0)),
        ],
        out_specs=[pl.BlockSpec((tc, d), lambda i: (i, 0)), pl.BlockSpec((tc, d), lambda i: (i, 0))],
        out_shape=[jax.ShapeDtypeStruct((t, d), F32), jax.ShapeDtypeStruct((t, d), BF16)],
        scratch_shapes=[pltpu.VMEM((TOP_K, tc, w), U32), pltpu.SemaphoreType.DMA(())],
        compiler_params=_cparams(("arbitrary",)),
        name="combine_ln",
    )(pos, ys, gate_tok, x, wsgu, wsd, g, b)


def _swap_rope_cols(w):
    half = w.shape[-1] // 2
    return jnp.concatenate([-w[..., half:], w[..., :half]], axis=-1)


def _prep_w_in(w_in):
    off_kr = Q_RANK + KV_RANK
    w_kr = w_in[:, off_kr:off_kr + ROPE]
    return jnp.concatenate(
        [w_in[:, :off_kr], w_kr, _swap_rope_cols(w_kr), w_in[:, off_kr + ROPE:]], axis=1).astype(BF16)


def _prep_w_uq(w_uq):
    w = w_uq.reshape(Q_RANK, N_HEADS, NOPE + ROPE)
    w_r = w[..., NOPE:]
    return jnp.concatenate([w[..., :NOPE], w_r, _swap_rope_cols(w_r)], axis=-1).reshape(
        Q_RANK, N_HEADS * HEAD_W).astype(BF16)


def kernel(x, positions, w_in, q_norm_g, kv_norm_g, w_uq, w_ukv, ssm_lambda_re, ssm_lambda_im, ssm_log_dt,
           ssm_b_re, ssm_b_im, ssm_c_re, ssm_c_im, ssm_d, w_glu, b_glu, attn_out_norm_g, ssm_out_norm_g,
           w_out, ln1_g, ln1_b, w_router, router_bias, w_gate, w_up, w_down, ws_gate, ws_up, ws_down,
           ln2_g, ln2_b):
    batch, seq, d_model = x.shape
    assert batch == 1
    depth = w_in.shape[0]
    alpha = float((2 * depth) ** 0.25)
    u_col_block = (Q_RANK + KV_RANK + LANES) // LANES

    inv_freq = ROPE_THETA ** (-(jnp.arange(0, ROPE, 2, dtype=F32) / ROPE))
    ang = positions.astype(F32)[0][:, None] * inv_freq
    cos, sin = jnp.cos(ang), jnp.sin(ang)
    cs = jnp.concatenate([cos, cos, sin, sin], axis=1)

    xf = x[0]
    xb = xf.astype(BF16)
    for l in range(depth):
        proj = _matmul(xb, _prep_w_in(w_in[l]), F32, tm=512, tn=1152)
        q, k, v = _mla_prep(proj, cs, q_norm_g[l][None], kv_norm_g[l][None],
                            _prep_w_uq(w_uq[l]), w_ukv[l].astype(BF16))
        attn = _flash(q, k, v)
        kc, win, wout, al = _s5_params(ssm_lambda_re[l], ssm_lambda_im[l], ssm_log_dt[l], ssm_b_re[l],
                                       ssm_b_im[l], ssm_c_re[l], ssm_c_im[l], ssm_d[l])
        y = _s5(proj, u_col_block, kc, win, wout, al)
        heads = _heads(attn, y, attn_out_norm_g[l][None], ssm_out_norm_g[l][None],
                       w_glu[l].astype(BF16), b_glu[l][None])
        x1, x1p = _outproj_ln(heads, w_out[l].astype(BF16), xf, ln1_g[l][None], ln1_b[l][None], alpha)
        wr_t = w_router[l].T
        wr_h = wr_t.astype(BF16)
        wr_l = (wr_t - wr_h.astype(F32)).astype(BF16)
        top_e, gate, rank, cnt = _router(x1, wr_h, wr_l, router_bias[l][:, None])
        starts, sched = _expert_schedule(cnt[:, 0], seq * TOP_K, min(MOE_BM, seq * TOP_K))
        pos = starts[top_e] + rank
        xs = _dispatch(pos, x1p)
        wgu = jnp.concatenate([w_gate[l], w_up[l]], axis=-1).astype(BF16)
        ys = _experts(sched, xs, wgu, w_down[l].astype(BF16), min(MOE_BM, seq * TOP_K))
        wsgu = jnp.concatenate([ws_gate[l], ws_up[l]], axis=-1).astype(BF16)
        xf, xb = _combine_ln(pos, ys, gate.T, x1, wsgu, ws_down[l].astype(BF16),
                             ln2_g[l][None], ln2_b[l][None], alpha)
    return xf[None]
```

```python
import functools
import math

import jax
import jax.numpy as jnp
from jax import lax
from jax.experimental import pallas as pl
from jax.experimental.pallas import tpu as pltpu

F32 = jnp.float32
BF16 = jnp.bfloat16
I32 = jnp.int32
U32 = jnp.uint32

N_HEADS = 16
NOPE = 128
ROPE = 64
V_DIM = 128
HEAD_W = 256
Q_RANK = 768
KV_RANK = 512
ROPE_THETA = 10000.0
SSM_H = 16
SSM_P = 64
N_EXPERTS = 64
D_FF = 256
TOP_K = 8
N_GROUPS = 8
TOPK_GROUPS = 4
ROUTED_SCALE = 2.5
LN_EPS = 1e-5
RMS_EPS = 1e-6

LANES = 128
V7X_VMEM_LIMIT = 56 * 1024 * 1024
SSM_L = 16
SLAB_G = LANES // SSM_H
SLAB_STATE = SLAB_G * SSM_P
MOE_BM = 256
MLA_TM = 256


def _cparams(sem):
    return pltpu.CompilerParams(dimension_semantics=sem, vmem_limit_bytes=V7X_VMEM_LIMIT)


def _sigmoid(z):
    return 1.0 / (1.0 + jnp.exp(-z))


def _pack_bf16_pair(lo, hi):
    lo_bits = lax.bitcast_convert_type(lo.astype(BF16).astype(F32), U32)
    hi_bits = lax.bitcast_convert_type(hi.astype(BF16).astype(F32), U32)
    return (lo_bits >> 16) | (hi_bits & jnp.uint32(0xFFFF0000))


def _unpack_bf16_pair(w):
    lo = lax.bitcast_convert_type(w << 16, F32)
    hi = lax.bitcast_convert_type(w & jnp.uint32(0xFFFF0000), F32)
    return lo, hi


def _mm_kernel(a_ref, b_ref, o_ref):
    o_ref[...] = jnp.dot(a_ref[...], b_ref[...], preferred_element_type=F32).astype(o_ref.dtype)


def _matmul(a, b, out_dtype, tm, tn):
    m, k = a.shape
    n = b.shape[1]
    tm, tn = min(tm, m), min(tn, n)
    return pl.pallas_call(
        _mm_kernel,
        grid=(n // tn, m // tm),
        in_specs=[pl.BlockSpec((tm, k), lambda j, i: (i, 0)), pl.BlockSpec((k, tn), lambda j, i: (0, j))],
        out_specs=pl.BlockSpec((tm, tn), lambda j, i: (i, j)),
        out_shape=jax.ShapeDtypeStruct((m, n), out_dtype),
        compiler_params=_cparams(("arbitrary", "arbitrary")),
        name="proj_in",
    )(a, b)


def _rope_pair(t):
    r = t + pltpu.roll(t, 64, axis=1)
    lane = lax.broadcasted_iota(I32, r.shape, 1)
    return jnp.where(lane < ROPE, r, 0.0)


def _mla_prep_kernel(c_ref, cs_ref, qg_ref, kvg_ref, wq_ref, wkt_ref, wv_ref, q_ref, kt_ref, vp_ref, *, scale):
    cq = c_ref[:, 0:Q_RANK]
    ckv = c_ref[:, Q_RANK:Q_RANK + KV_RANK]
    kblk = c_ref[:, Q_RANK + KV_RANK:Q_RANK + KV_RANK + LANES]
    cs = cs_ref[...]
    cqn = (cq * lax.rsqrt(jnp.mean(cq * cq, -1, keepdims=True) + RMS_EPS) * qg_ref[...]).astype(BF16)
    ckvn = (ckv * lax.rsqrt(jnp.mean(ckv * ckv, -1, keepdims=True) + RMS_EPS) * kvg_ref[...]).astype(BF16)
    k_rope_t = jnp.transpose(_rope_pair(kblk * cs)).astype(BF16)
    lane = lax.broadcasted_iota(I32, (cq.shape[0], LANES), 1)
    ones_col = jnp.where(lane == 0, 1.0, 0.0).astype(BF16)
    nt = (((1,), (1,)), ((), ()))
    for h in range(N_HEADS):
        qh = jnp.dot(cqn, wq_ref[:, h * HEAD_W:(h + 1) * HEAD_W], preferred_element_type=F32)
        q_ref[:, h * HEAD_W:h * HEAD_W + NOPE] = (qh[:, :NOPE] * scale).astype(BF16)
        q_ref[:, h * HEAD_W + NOPE:(h + 1) * HEAD_W] = (_rope_pair(qh[:, NOPE:] * cs) * scale).astype(BF16)
        kt = lax.dot_general(wkt_ref[h * NOPE:(h + 1) * NOPE, :], ckvn, nt, preferred_element_type=F32)
        kt_ref[h * HEAD_W:h * HEAD_W + NOPE, :] = kt.astype(BF16)
        kt_ref[h * HEAD_W + NOPE:(h + 1) * HEAD_W, :] = k_rope_t
        vh = jnp.dot(ckvn, wv_ref[:, h * V_DIM:(h + 1) * V_DIM], preferred_element_type=F32)
        vp_ref[:, h * HEAD_W:h * HEAD_W + V_DIM] = vh.astype(BF16)
        vp_ref[:, h * HEAD_W + V_DIM:(h + 1) * HEAD_W] = ones_col


def _mla_prep(proj, cs, qg, kvg, wq, wkt, wv, tm):
    s = proj.shape[0]
    cw = Q_RANK + KV_RANK + LANES
    scale = float((NOPE + ROPE) ** -0.5 * math.log2(math.e))
    hw = N_HEADS * HEAD_W
    return pl.pallas_call(
        functools.partial(_mla_prep_kernel, scale=scale),
        grid=(s // tm,),
        in_specs=[
            pl.BlockSpec((tm, cw), lambda i: (i, 0)),
            pl.BlockSpec((tm, LANES), lambda i: (i, 0)),
            pl.BlockSpec((1, Q_RANK), lambda i: (0, 0)),
            pl.BlockSpec((1, KV_RANK), lambda i: (0, 0)),
            pl.BlockSpec((Q_RANK, hw), lambda i: (0, 0)),
            pl.BlockSpec((N_HEADS * NOPE, KV_RANK), lambda i: (0, 0)),
            pl.BlockSpec((KV_RANK, N_HEADS * V_DIM), lambda i: (0, 0)),
        ],
        out_specs=[
            pl.BlockSpec((tm, hw), lambda i: (i, 0)),
            pl.BlockSpec((None, hw, tm), lambda i: (i, 0, 0)),
            pl.BlockSpec((tm, hw), lambda i: (i, 0)),
        ],
        out_shape=[
            jax.ShapeDtypeStruct((s, hw), BF16),
            jax.ShapeDtypeStruct((s // tm, hw, tm), BF16),
            jax.ShapeDtypeStruct((s, hw), BF16),
        ],
        compiler_params=_cparams(("arbitrary",)),
        name="mla_prep",
    )(proj, cs, qg, kvg, wq, wkt, wv)


def _flash_kernel(q_ref, k_ref, v_ref, o_ref, s_scr, m_scr, acc_scr, *, t, ku):
    i = pl.program_id(1)
    m_scr[...] = jnp.full(m_scr.shape, -jnp.inf, F32)
    acc_scr[...] = jnp.zeros(acc_scr.shape, F32)
    nsub = t // ku
    ncol = ku // LANES

    def scores(j, slot):
        q = q_ref[...]
        for u in range(nsub):
            su = jnp.dot(q, k_ref[j * nsub + u], preferred_element_type=F32)
            for c in range(ncol):
                s_scr[slot, u * ncol + c] = su[:, c * LANES:(c + 1) * LANES]

    def consume(j, slot, masked):
        cols = [s_scr[slot, c] for c in range(nsub * ncol)]
        if masked:
            row = lax.broadcasted_iota(I32, (t, LANES), 0)
            lane = lax.broadcasted_iota(I32, (t, LANES), 1)
            cols = [jnp.where(c * LANES + lane <= row, sc, -jnp.inf) for c, sc in enumerate(cols)]
        m_cur = cols[0]
        for sc in cols[1:]:
            m_cur = jnp.maximum(m_cur, sc)
        m_prev = m_scr[...]
        m_new = jnp.maximum(m_prev, jnp.max(m_cur, axis=-1, keepdims=True))
        alpha = jnp.exp2(m_prev - m_new)
        p = jnp.concatenate([jnp.exp2(sc - m_new).astype(BF16) for sc in cols], axis=1)
        v = v_ref[pl.ds(pl.multiple_of(j * t, t), t), :]
        pv = jnp.dot(p, v, preferred_element_type=F32)
        acc_scr[...] = jnp.concatenate([alpha, alpha], axis=1) * acc_scr[...] + pv
        m_scr[...] = m_new

    scores(0, 0)

    def pair(jj, carry):
        scores(2 * jj + 1, 1)
        consume(2 * jj, 0, False)
        scores(2 * jj + 2, 0)
        consume(2 * jj + 1, 1, False)
        return carry

    lax.fori_loop(0, i // 2, pair, 0)

    @pl.when(i % 2 == 1)
    def _():
        scores(i, 1)
        consume(i - 1, 0, False)
        consume(i, 1, True)

    @pl.when(i % 2 == 0)
    def _():
        consume(i, 0, True)

    acc = acc_scr[...]
    o_ref[...] = acc[:, :V_DIM] / acc[:, V_DIM:V_DIM + 1]


def _flash(q, kt, vp, t=512):
    s = q.shape[0]
    ku = kt.shape[2]
    t = min(t, s)
    return pl.pallas_call(
        functools.partial(_flash_kernel, t=t, ku=ku),
        grid=(N_HEADS, s // t),
        in_specs=[
            pl.BlockSpec((t, HEAD_W), lambda h, i: (i, h)),
            pl.BlockSpec((s // ku, HEAD_W, ku), lambda h, i: (0, h, 0)),
            pl.BlockSpec((s, HEAD_W), lambda h, i: (0, h)),
        ],
        out_specs=pl.BlockSpec((t, V_DIM), lambda h, i: (i, h)),
        out_shape=jax.ShapeDtypeStruct((s, N_HEADS * V_DIM), F32),
        scratch_shapes=[pltpu.VMEM((2, t // LANES, t, LANES), F32), pltpu.VMEM((t, LANES), F32),
                        pltpu.VMEM((t, HEAD_W), F32)],
        compiler_params=_cparams(("arbitrary", "arbitrary")),
        name="flash_attn",
    )(q, kt, vp)


def _s5_kernel(u_ref, kc_ref, win_ref, wout_ref, al_ref, y_ref, t_scr, ucat_scr, sc_scr, h_scr, *, nct):
    @pl.when(pl.program_id(1) == 0)
    def _():
        t_scr[...] = jnp.zeros(t_scr.shape, BF16)
        for s in range(SSM_L):
            for s2 in range(s, SSM_L):
                t_scr[s * LANES:(s + 1) * LANES, s2 * LANES:(s2 + 1) * LANES] = kc_ref[s2 - s]
        h_scr[...] = jnp.zeros(h_scr.shape, F32)

    for s in range(SSM_L):
        ucat_scr[:, s * LANES:(s + 1) * LANES] = u_ref[pl.ds(s, nct, stride=SSM_L), :].astype(BF16)
    ucat = ucat_scr[...]
    sc_scr[...] = jnp.dot(ucat, win_ref[...], preferred_element_type=F32)
    a_re = al_ref[:, :SLAB_STATE]
    a_im = al_ref[:, SLAB_STATE:]

    def step(c, carry):
        h_re, h_im = carry
        row = sc_scr[pl.ds(c, 1), :]
        sc_scr[pl.ds(c, 1), :] = jnp.concatenate([h_re, h_im], axis=1)
        n_re = a_re * h_re - a_im * h_im + row[:, :SLAB_STATE]
        n_im = a_re * h_im + a_im * h_re + row[:, SLAB_STATE:]
        return n_re, n_im

    h0 = h_scr[...]
    h_re, h_im = lax.fori_loop(0, nct, step, (h0[:, :SLAB_STATE], h0[:, SLAB_STATE:]))
    h_scr[...] = jnp.concatenate([h_re, h_im], axis=1)
    y = (jnp.dot(ucat, t_scr[...], preferred_element_type=F32)
         + jnp.dot(sc_scr[...].astype(BF16), wout_ref[...], preferred_element_type=F32))
    for s in range(SSM_L):
        y_ref[pl.ds(s, nct, stride=SSM_L), :] = y[:, s * LANES:(s + 1) * LANES]


def _s5(proj, u_col_block, kc, win, wout, al, ts=8192):
    s = proj.shape[0]
    ts = min(ts, s)
    nct = ts // SSM_L
    n_slab = kc.shape[0]
    lw = SSM_L * LANES
    return pl.pallas_call(
        functools.partial(_s5_kernel, nct=nct),
        grid=(n_slab, s // ts),
        in_specs=[
            pl.BlockSpec((ts, LANES), lambda j, t: (t, u_col_block + j)),
            pl.BlockSpec((None, SSM_L, LANES, LANES), lambda j, t: (j, 0, 0, 0)),
            pl.BlockSpec((None, lw, 2 * SLAB_STATE), lambda j, t: (j, 0, 0)),
            pl.BlockSpec((None, 2 * SLAB_STATE, lw), lambda j, t: (j, 0, 0)),
            pl.BlockSpec((None, 1, 2 * SLAB_STATE), lambda j, t: (j, 0, 0)),
        ],
        out_specs=pl.BlockSpec((ts, LANES), lambda j, t: (t, j)),
        out_shape=jax.ShapeDtypeStruct((s, n_slab * LANES), F32),
        scratch_shapes=[
            pltpu.VMEM((lw, lw), BF16),
            pltpu.VMEM((nct, lw), BF16),
            pltpu.VMEM((nct, 2 * SLAB_STATE), F32),
            pltpu.VMEM((1, 2 * SLAB_STATE), F32),
        ],
        compiler_params=_cparams(("arbitrary", "arbitrary")),
        name="s5",
    )(proj, kc, win, wout, al)


def _s5_params(lam_re, lam_im, log_dt, b_re, b_im, c_re, c_im, d_skip):
    g, p = lam_re.shape
    h = b_re.shape[-1]
    n_slab = g // SLAB_G
    hp = lax.Precision.HIGHEST
    lr = jnp.minimum(lam_re.astype(F32), -1e-4)
    li = lam_im.astype(F32)
    dt = jnp.exp(log_dt.astype(F32))[:, None]
    kk = jnp.arange(SSM_L + 1, dtype=F32)[:, None, None]
    mag = jnp.exp(lr * dt * kk)
    ang = li * dt * kk
    pw_re, pw_im = mag * jnp.cos(ang), mag * jnp.sin(ang)
    x, y = pw_re[1] - 1.0, pw_im[1]
    den = lr * lr + li * li
    f_re, f_im = (x * lr + y * li) / den, (y * lr - x * li) / den
    bb_re = f_re[..., None] * b_re - f_im[..., None] * b_im
    bb_im = f_re[..., None] * b_im + f_im[..., None] * b_re
    cp_re = c_re[None] * pw_re[:SSM_L, :, None, :] - c_im[None] * pw_im[:SSM_L, :, None, :]
    cp_im = c_re[None] * pw_im[:SSM_L, :, None, :] + c_im[None] * pw_re[:SSM_L, :, None, :]
    kmat = (jnp.einsum('kgop,gpi->gkoi', cp_re, bb_re, precision=hp)
            - jnp.einsum('kgop,gpi->gkoi', cp_im, bb_im, precision=hp))
    kmat = kmat.at[:, 0].add(jnp.eye(h, dtype=F32)[None] * d_skip.astype(F32)[:, :, None])
    eye = jnp.eye(SLAB_G, dtype=F32)
    kc = jnp.einsum('jakoi,ab->jkaibo', kmat.reshape(n_slab, SLAB_G, SSM_L, h, h), eye)
    kc = kc.reshape(n_slab, SSM_L, LANES, LANES).astype(BF16)
    rk = (SSM_L - 1) - jnp.arange(SSM_L, dtype=F32)[:, None, None]
    rmag, rang = jnp.exp(lr * dt * rk), li * dt * rk
    rp_re, rp_im = rmag * jnp.cos(rang), rmag * jnp.sin(rang)
    wi_re = rp_re[:, :, None, :] * bb_re.transpose(0, 2, 1)[None] - rp_im[:, :, None, :] * bb_im.transpose(0, 2, 1)[None]
    wi_im = rp_re[:, :, None, :] * bb_im.transpose(0, 2, 1)[None] + rp_im[:, :, None, :] * bb_re.transpose(0, 2, 1)[None]

    def slab_in(w):
        w = w.reshape(SSM_L, n_slab, SLAB_G, h, p)
        return jnp.einsum('sjaip,ab->jsaibp', w, eye).reshape(n_slab, SSM_L * LANES, SLAB_STATE)

    win = jnp.concatenate([slab_in(wi_re), slab_in(wi_im)], axis=-1).astype(BF16)
    q_re, q_im = pw_re[1:SSM_L + 1], pw_im[1:SSM_L + 1]
    cl_re = c_re[None] * q_re[:, :, None, :] - c_im[None] * q_im[:, :, None, :]
    cl_im = c_re[None] * q_im[:, :, None, :] + c_im[None] * q_re[:, :, None, :]

    def slab_out(w):
        w = w.reshape(SSM_L, n_slab, SLAB_G, h, p)
        return jnp.einsum('sjaop,ab->japsbo', w, eye).reshape(n_slab, SLAB_STATE, SSM_L * LANES)

    wout = jnp.concatenate([slab_out(cl_re), slab_out(-cl_im)], axis=1).astype(BF16)
    al = jnp.concatenate([pw_re[SSM_L].reshape(n_slab, 1, SLAB_STATE),
                          pw_im[SSM_L].reshape(n_slab, 1, SLAB_STATE)], axis=-1)
    return kc, win, wout, al


def _heads_kernel(attn_ref, y_ref, ag_ref, sg_ref, wglu_ref, bglu_ref, o_ref):
    a = attn_ref[...]
    da = a.shape[1]
    o_ref[:, :da] = (a * lax.rsqrt(jnp.mean(a * a, -1, keepdims=True) + RMS_EPS) * ag_ref[...]).astype(BF16)
    g = jax.nn.gelu(y_ref[...])
    z = jnp.dot(g.astype(BF16), wglu_ref[...], preferred_element_type=F32) + bglu_ref[...]
    ssm = g * _sigmoid(z)
    o_ref[:, da:] = (ssm * lax.rsqrt(jnp.mean(ssm * ssm, -1, keepdims=True) + RMS_EPS) * sg_ref[...]).astype(BF16)


def _heads(attn, y, ag, sg, wglu, bglu, tm=512):
    s, da = attn.shape
    ds = y.shape[1]
    tm = min(tm, s)
    return pl.pallas_call(
        _heads_kernel,
        grid=(s // tm,),
        in_specs=[
            pl.BlockSpec((tm, da), lambda i: (i, 0)),
            pl.BlockSpec((tm, ds), lambda i: (i, 0)),
            pl.BlockSpec((1, da), lambda i: (0, 0)),
            pl.BlockSpec((1, ds), lambda i: (0, 0)),
            pl.BlockSpec((ds, ds), lambda i: (0, 0)),
            pl.BlockSpec((1, ds), lambda i: (0, 0)),
        ],
        out_specs=pl.BlockSpec((tm, da + ds), lambda i: (i, 0)),
        out_shape=jax.ShapeDtypeStruct((s, da + ds), BF16),
        compiler_params=_cparams(("arbitrary",)),
        name="heads",
    )(attn, y, ag, sg, wglu, bglu)


def _layernorm_rows(x, g, b):
    mu = jnp.mean(x, -1, keepdims=True)
    xc = x - mu
    var = jnp.mean(xc * xc, -1, keepdims=True)
    return xc * lax.rsqrt(var + LN_EPS) * g + b


def _outproj_kernel(h_ref, w_ref, x_ref, g_ref, b_ref, o_ref, p_ref, *, alpha, tn, nj):
    j = pl.program_id(1)
    pre = alpha * x_ref[...] + jnp.dot(h_ref[...], w_ref[...], preferred_element_type=F32)
    for jj in range(nj):
        @pl.when(j == jj)
        def _(jj=jj):
            o_ref[:, jj * tn:(jj + 1) * tn] = pre

    @pl.when(j == nj - 1)
    def _():
        xn = _layernorm_rows(o_ref[...], g_ref[...], b_ref[...])
        o_ref[...] = xn
        half = xn.shape[1] // 2
        p_ref[...] = _pack_bf16_pair(xn[:, :half], xn[:, half:])


def _outproj_ln(heads, w, x, g, b, alpha, tm=512, tn=512):
    s, k = heads.shape
    n = w.shape[1]
    tm, tn = min(tm, s), min(tn, n)
    nj = n // tn
    return pl.pallas_call(
        functools.partial(_outproj_kernel, alpha=alpha, tn=tn, nj=nj),
        grid=(s // tm, nj),
        in_specs=[
            pl.BlockSpec((tm, k), lambda i, j: (i, 0)),
            pl.BlockSpec((k, tn), lambda i, j: (0, j)),
            pl.BlockSpec((tm, tn), lambda i, j: (i, j)),
            pl.BlockSpec((1, n), lambda i, j: (0, 0)),
            pl.BlockSpec((1, n), lambda i, j: (0, 0)),
        ],
        out_specs=[pl.BlockSpec((tm, n), lambda i, j: (i, 0)), pl.BlockSpec((tm, n // 2), lambda i, j: (i, 0))],
        out_shape=[jax.ShapeDtypeStruct((s, n), F32), jax.ShapeDtypeStruct((s, n // 2), U32)],
        compiler_params=_cparams(("arbitrary", "arbitrary")),
        name="outproj_ln",
    )(heads, w, x, g, b)


def _first_argmax(v, iota, n):
    m = jnp.max(v, axis=0, keepdims=True)
    first = jnp.min(jnp.where(v == m, iota, n), axis=0, keepdims=True)
    return m, first


def _router_kernel(x_ref, wh_ref, wl_ref, bias_ref, tri_ref, e_ref, g_ref, r_ref, cnt_ref, carry_scr):
    @pl.when(pl.program_id(0) == 0)
    def _():
        carry_scr[...] = jnp.zeros(carry_scr.shape, F32)

    x = x_ref[...]
    xh = x.astype(BF16)
    xl = (x - xh.astype(F32)).astype(BF16)
    dn = (((1,), (1,)), ((), ()))
    wh, wl = wh_ref[...], wl_ref[...]
    logits = (lax.dot_general(wh, xh, dn, preferred_element_type=F32)
              + lax.dot_general(wh, xl, dn, preferred_element_type=F32)
              + lax.dot_general(wl, xh, dn, preferred_element_type=F32))
    scores = _sigmoid(logits)
    sel = scores + bias_ref[...]
    t = sel.shape[1]
    per = N_EXPERTS // N_GROUPS
    ninf = -jnp.inf
    iota_g = lax.broadcasted_iota(I32, (per, t), 0)
    gs_rows = []
    for gi in range(N_GROUPS):
        blk = sel[gi * per:(gi + 1) * per, :]
        m1, f1 = _first_argmax(blk, iota_g, per)
        m2 = jnp.max(jnp.where(iota_g == f1, ninf, blk), axis=0, keepdims=True)
        gs_rows.append(m1 + m2)
    gs = jnp.concatenate(gs_rows, axis=0)
    iota_ng = lax.broadcasted_iota(I32, (N_GROUPS, t), 0)
    gmask = jnp.zeros((N_GROUPS, t), I32)
    for _ in range(TOPK_GROUPS):
        _, f = _first_argmax(gs, iota_ng, N_GROUPS)
        pick = iota_ng == f
        gmask = jnp.where(pick, 1, gmask)
        gs = jnp.where(pick, ninf, gs)
    selm = jnp.concatenate(
        [jnp.where(gmask[gi:gi + 1, :] > 0, sel[gi * per:(gi + 1) * per, :], ninf) for gi in range(N_GROUPS)], axis=0)
    iota_e = lax.broadcasted_iota(I32, (N_EXPERTS, t), 0)
    onehot = jnp.zeros((N_EXPERTS, t), F32)
    e_rows, g_rows = [], []
    for _ in range(TOP_K):
        _, f = _first_argmax(selm, iota_e, N_EXPERTS)
        pick = iota_e == f
        e_rows.append(f)
        g_rows.append(jnp.sum(jnp.where(pick, scores, 0.0), axis=0, keepdims=True))
        onehot = jnp.where(pick, 1.0, onehot)
        selm = jnp.where(pick, ninf, selm)
    gate = jnp.concatenate(g_rows, axis=0)
    gate = gate / jnp.sum(gate, axis=0, keepdims=True) * ROUTED_SCALE
    cum = jnp.dot(onehot.astype(BF16), tri_ref[...], preferred_element_type=F32)
    rank_e = cum - onehot + carry_scr[:, 0:1]
    r_rows = [jnp.sum(jnp.where(iota_e == f, rank_e, 0.0), axis=0, keepdims=True) for f in e_rows]
    carry = carry_scr[...] + cum[:, t - 1:t]
    carry_scr[...] = carry
    e_ref[...] = jnp.concatenate(e_rows, axis=0)
    g_ref[...] = gate
    r_ref[...] = jnp.concatenate(r_rows, axis=0).astype(I32)
    cnt_ref[...] = carry


def _router(x, wh, wl, bias, tm=512):
    t, d = x.shape
    tm = min(tm, t)
    tri = (jnp.arange(tm)[:, None] <= jnp.arange(tm)[None, :]).astype(BF16)
    return pl.pallas_call(
        _router_kernel,
        grid=(t // tm,),
        in_specs=[
            pl.BlockSpec((tm, d), lambda i: (i, 0)),
            pl.BlockSpec((N_EXPERTS, d), lambda i: (0, 0)),
            pl.BlockSpec((N_EXPERTS, d), lambda i: (0, 0)),
            pl.BlockSpec((N_EXPERTS, 1), lambda i: (0, 0)),
            pl.BlockSpec((tm, tm), lambda i: (0, 0)),
        ],
        out_specs=[
            pl.BlockSpec((TOP_K, tm), lambda i: (0, i)),
            pl.BlockSpec((TOP_K, tm), lambda i: (0, i)),
            pl.BlockSpec((TOP_K, tm), lambda i: (0, i)),
            pl.BlockSpec((N_EXPERTS, LANES), lambda i: (0, 0)),
        ],
        out_shape=[
            jax.ShapeDtypeStruct((TOP_K, t), I32),
            jax.ShapeDtypeStruct((TOP_K, t), F32),
            jax.ShapeDtypeStruct((TOP_K, t), I32),
            jax.ShapeDtypeStruct((N_EXPERTS, LANES), F32),
        ],
        scratch_shapes=[pltpu.VMEM((N_EXPERTS, LANES), F32)],
        compiler_params=_cparams(("arbitrary",)),
        name="router",
    )(x, wh, wl, bias, tri)


def _dispatch_kernel(pos_ref, x_ref, xs_hbm, sem, *, tt):
    def issue(t, carry):
        for k in range(TOP_K):
            pltpu.make_async_copy(x_ref.at[pl.ds(t, 1)], xs_hbm.at[pl.ds(pos_ref[k, t], 1)], sem).start()
        return carry

    lax.fori_loop(0, tt, issue, 0)
    for k in range(TOP_K):
        pltpu.make_async_copy(x_ref, xs_hbm.at[pl.ds(0, tt)], sem).wait()


def _dispatch(pos, xp, tt=512):
    t, w = xp.shape
    tt = min(tt, t)
    return pl.pallas_call(
        functools.partial(_dispatch_kernel, tt=tt),
        grid=(t // tt,),
        in_specs=[
            pl.BlockSpec((TOP_K, tt), lambda i: (0, i), memory_space=pltpu.SMEM),
            pl.BlockSpec((tt, w), lambda i: (i, 0)),
        ],
        out_specs=pl.BlockSpec(memory_space=pl.ANY),
        out_shape=jax.ShapeDtypeStruct((t * TOP_K, w), U32),
        scratch_shapes=[pltpu.SemaphoreType.DMA(())],
        compiler_params=_cparams(("arbitrary",)),
        name="dispatch",
    )(pos, xp)


def _experts_kernel(vblk_ref, vexp_ref, vlo_ref, vhi_ref, vfirst_ref, xs_ref, wgu_ref, wd_ref, ys_ref, *, bm):
    v = pl.program_id(0)
    lo, hi = vlo_ref[v], vhi_ref[v]

    @pl.when(hi > lo)
    def _():
        xa, xb = _unpack_bf16_pair(xs_ref[...])
        half = xa.shape[1]
        gu = (jnp.dot(xa.astype(BF16), wgu_ref[:half, :], preferred_element_type=F32)
              + jnp.dot(xb.astype(BF16), wgu_ref[half:, :], preferred_element_type=F32))
        hmid = (gu[:, :D_FF] * _sigmoid(gu[:, :D_FF])) * gu[:, D_FF:]
        y = jnp.dot(hmid.astype(BF16), wd_ref[...], preferred_element_type=F32)
        packed = _pack_bf16_pair(y[:, :half], y[:, half:])
        rows = vblk_ref[v] * bm + lax.broadcasted_iota(I32, packed.shape, 0)
        mine = (rows >= lo) & (rows < hi)

        @pl.when(vfirst_ref[v] == 1)
        def _():
            ys_ref[...] = jnp.where(mine, packed, jnp.uint32(0))

        @pl.when(vfirst_ref[v] == 0)
        def _():
            ys_ref[...] = jnp.where(mine, packed, ys_ref[...])


def _experts(sched, xs, wgu, wd, bm):
    r, w = xs.shape
    d = wd.shape[2]
    nv = sched[0].shape[0]
    grid_spec = pltpu.PrefetchScalarGridSpec(
        num_scalar_prefetch=5,
        grid=(nv,),
        in_specs=[
            pl.BlockSpec((bm, w), lambda v, vb, ve, lo, hi, fi: (vb[v], 0)),
            pl.BlockSpec((None, d, 2 * D_FF), lambda v, vb, ve, lo, hi, fi: (ve[v], 0, 0)),
            pl.BlockSpec((None, D_FF, d), lambda v, vb, ve, lo, hi, fi: (ve[v], 0, 0)),
        ],
        out_specs=pl.BlockSpec((bm, w), lambda v, vb, ve, lo, hi, fi: (vb[v], 0)),
    )
    return pl.pallas_call(
        functools.partial(_experts_kernel, bm=bm),
        grid_spec=grid_spec,
        out_shape=jax.ShapeDtypeStruct((r, w), U32),
        compiler_params=_cparams(("arbitrary",)),
        name="experts",
    )(*sched, xs, wgu, wd)


def _expert_schedule(counts, n_rows, bm):
    counts = counts.astype(I32)
    ends = jnp.cumsum(counts)
    starts = ends - counts
    nb = n_rows // bm
    nv = nb + N_EXPERTS - 1
    first_blk = starts // bm
    last_blk = jnp.maximum(ends - 1, 0) // bm
    nvis = jnp.where(counts > 0, last_blk - first_blk + 1, 0)
    vis_end = jnp.cumsum(nvis)
    vis_start = vis_end - nvis
    total = vis_end[-1]
    v = jnp.arange(nv, dtype=I32)
    e = jnp.minimum(jnp.sum((vis_end[None, :] <= v[:, None]).astype(I32), axis=1), N_EXPERTS - 1)
    blk = first_blk[e] + (v - vis_start[e])
    real = v < total
    last_v = jnp.maximum(total - 1, 0)
    e = jnp.where(real, e, e[last_v])
    blk = jnp.where(real, blk, blk[last_v]).astype(I32)
    lo = jnp.where(real, starts[e], 0).astype(I32)
    hi = jnp.where(real, ends[e], 0).astype(I32)
    prev_blk = jnp.concatenate([jnp.full((1,), -1, I32), blk[:-1]])
    first = (real & (blk != prev_blk)).astype(I32)
    return starts, (blk, e, lo, hi, first)


def _combine_kernel(pos_ref, ys_hbm, gate_ref, x_ref, wsgu_ref, wsd_ref, g_ref, b_ref, o_ref, ob_ref,
                    buf, sem, *, tc, alpha):
    def issue(t, carry):
        for k in range(TOP_K):
            pltpu.make_async_copy(ys_hbm.at[pl.ds(pos_ref[k, t], 1)], buf.at[k, pl.ds(t, 1)], sem).start()
        return carry

    lax.fori_loop(0, tc, issue, 0)
    x = x_ref[...]
    xb = x.astype(BF16)
    gu = jnp.dot(xb, wsgu_ref[...], preferred_element_type=F32)
    hmid = (gu[:, :D_FF] * _sigmoid(gu[:, :D_FF])) * gu[:, D_FF:]
    shared = jnp.dot(hmid.astype(BF16), wsd_ref[...], preferred_element_type=F32)
    for k in range(TOP_K):
        pltpu.make_async_copy(ys_hbm.at[pl.ds(0, tc)], buf.at[k], sem).wait()
    half = x.shape[1] // 2
    acc_lo = shared[:, :half]
    acc_hi = shared[:, half:]
    gate = gate_ref[...]
    for k in range(TOP_K):
        lo, hi = _unpack_bf16_pair(buf[k])
        gk = gate[:, k:k + 1]
        acc_lo = acc_lo + gk * lo
        acc_hi = acc_hi + gk * hi
    pre = alpha * x + jnp.concatenate([acc_lo, acc_hi], axis=1)
    xn = _layernorm_rows(pre, g_ref[...], b_ref[...])
    o_ref[...] = xn
    ob_ref[...] = xn.astype(BF16)


def _combine_ln(pos, ys, gate_tok, x, wsgu, wsd, g, b, alpha, tc=256):
    t, d = x.shape
    tc = min(tc, t)
    w = ys.shape[1]
    return pl.pallas_call(
        functools.partial(_combine_kernel, tc=tc, alpha=alpha),
        grid=(t // tc,),
        in_specs=[
            pl.BlockSpec((TOP_K, tc), lambda i: (0, i), memory_space=pltpu.SMEM),
            pl.BlockSpec(memory_space=pl.ANY),
            pl.BlockSpec((tc, TOP_K), lambda i: (i, 0)),
            pl.BlockSpec((tc, d), lambda i: (i, 0)),
            pl.BlockSpec((d, 2 * D_FF), lambda i: (0, 0)),
            pl.BlockSpec((D_FF, d), lambda i: (0, 0)),
            pl.BlockSpec((1, d), lambda i: (0, 0)),
            pl.BlockSpec((1, d), lambda i: (0, 0)),
        ],
        out_specs=[pl.BlockSpec((tc, d), lambda i: (i, 0)), pl.BlockSpec((tc, d), lambda i: (i, 0))],
        out_shape=[jax.ShapeDtypeStruct((t, d), F32), jax.ShapeDtypeStruct((t, d), BF16)],
        scratch_shapes=[pltpu.VMEM((TOP_K, tc, w), U32), pltpu.SemaphoreType.DMA(())],
        compiler_params=_cparams(("arbitrary",)),
        name="combine_ln",
    )(pos, ys, gate_tok, x, wsgu, wsd, g, b)


def _swap_rope_cols(w):
    half = w.shape[-1] // 2
    return jnp.concatenate([-w[..., half:], w[..., :half]], axis=-1)


def _prep_w_in(w_in):
    off_kr = Q_RANK + KV_RANK
    w_kr = w_in[:, off_kr:off_kr + ROPE]
    return jnp.concatenate(
        [w_in[:, :off_kr], w_kr, _swap_rope_cols(w_kr), w_in[:, off_kr + ROPE:]], axis=1).astype(BF16)


def _prep_w_uq(w_uq):
    w = w_uq.reshape(Q_RANK, N_HEADS, NOPE + ROPE)
    w_r = w[..., NOPE:]
    return jnp.concatenate([w[..., :NOPE], w_r, _swap_rope_cols(w_r)], axis=-1).reshape(
        Q_RANK, N_HEADS * HEAD_W).astype(BF16)


def kernel(x, positions, w_in, q_norm_g, kv_norm_g, w_uq, w_ukv, ssm_lambda_re, ssm_lambda_im, ssm_log_dt,
           ssm_b_re, ssm_b_im, ssm_c_re, ssm_c_im, ssm_d, w_glu, b_glu, attn_out_norm_g, ssm_out_norm_g,
           w_out, ln1_g, ln1_b, w_router, router_bias, w_gate, w_up, w_down, ws_gate, ws_up, ws_down,
           ln2_g, ln2_b):
    batch, seq, d_model = x.shape
    assert batch == 1
    depth = w_in.shape[0]
    alpha = float((2 * depth) ** 0.25)
    u_col_block = (Q_RANK + KV_RANK + LANES) // LANES

    inv_freq = ROPE_THETA ** (-(jnp.arange(0, ROPE, 2, dtype=F32) / ROPE))
    ang = positions.astype(F32)[0][:, None] * inv_freq
    cos, sin = jnp.cos(ang), jnp.sin(ang)
    cs = jnp.concatenate([cos, cos, sin, sin], axis=1)

    xf = x[0]
    xb = xf.astype(BF16)
    for l in range(depth):
        proj = _matmul(xb, _prep_w_in(w_in[l]), F32, tm=512, tn=1152)
        w_kv = w_ukv[l].reshape(KV_RANK, N_HEADS, NOPE + V_DIM)
        wkt = w_kv[..., :NOPE].reshape(KV_RANK, N_HEADS * NOPE).T.astype(BF16)
        wv = w_kv[..., NOPE:].reshape(KV_RANK, N_HEADS * V_DIM).astype(BF16)
        q, kt, vp = _mla_prep(proj, cs, q_norm_g[l][None], kv_norm_g[l][None],
                              _prep_w_uq(w_uq[l]), wkt, wv, min(MLA_TM, seq))
        attn = _flash(q, kt, vp)
        kc, win, wout, al = _s5_params(ssm_lambda_re[l], ssm_lambda_im[l], ssm_log_dt[l], ssm_b_re[l],
                                       ssm_b_im[l], ssm_c_re[l], ssm_c_im[l], ssm_d[l])
        y = _s5(proj, u_col_block, kc, win, wout, al)
        heads = _heads(attn, y, attn_out_norm_g[l][None], ssm_out_norm_g[l][None],
                       w_glu[l].astype(BF16), b_glu[l][None])
        x1, x1p = _outproj_ln(heads, w_out[l].astype(BF16), xf, ln1_g[l][None], ln1_b[l][None], alpha)
        wr_t = w_router[l].T
        wr_h = wr_t.astype(BF16)
        wr_l = (wr_t - wr_h.astype(F32)).astype(BF16)
        top_e, gate, rank, cnt = _router(x1, wr_h, wr_l, router_bias[l][:, None])
        starts, sched = _expert_schedule(cnt[:, 0], seq * TOP_K, min(MOE_BM, seq * TOP_K))
        e_ids = jnp.arange(N_EXPERTS, dtype=I32)[:, None, None]
        pos = rank + jnp.sum(jnp.where(top_e[None] == e_ids, starts[:, None, None], 0), axis=0)
        xs = _dispatch(pos, x1p)
        wgu = jnp.concatenate([w_gate[l], w_up[l]], axis=-1).astype(BF16)
        ys = _experts(sched, xs, wgu, w_down[l].astype(BF16), min(MOE_BM, seq * TOP_K))
        wsgu = jnp.concatenate([ws_gate[l], ws_up[l]], axis=-1).astype(BF16)
        xf, xb = _combine_ln(pos, ys, gate.T, x1, wsgu, ws_down[l].astype(BF16),
                             ln2_g[l][None], ln2_b[l][None], alpha)
    return xf[None]
```

```python
import functools
import math

import jax
import jax.numpy as jnp
from jax import lax
from jax.experimental import pallas as pl
from jax.experimental.pallas import tpu as pltpu

F32 = jnp.float32
BF16 = jnp.bfloat16
I32 = jnp.int32
U32 = jnp.uint32

N_HEADS = 16
NOPE = 128
ROPE = 64
V_DIM = 128
HEAD_W = 256
Q_RANK = 768
KV_RANK = 512
ROPE_THETA = 10000.0
SSM_H = 16
SSM_P = 64
N_EXPERTS = 64
D_FF = 256
TOP_K = 8
N_GROUPS = 8
TOPK_GROUPS = 4
ROUTED_SCALE = 2.5
LN_EPS = 1e-5
RMS_EPS = 1e-6

LANES = 128
V7X_VMEM_LIMIT = 56 * 1024 * 1024
SSM_L = 16
SLAB_G = LANES // SSM_H
SLAB_STATE = SLAB_G * SSM_P
MOE_BM = 256
MLA_TM = 256


def _cparams(sem):
    return pltpu.CompilerParams(dimension_semantics=sem, vmem_limit_bytes=V7X_VMEM_LIMIT)


def _sigmoid(z):
    return 1.0 / (1.0 + jnp.exp(-z))


def _pack_bf16_pair(lo, hi):
    lo_bits = lax.bitcast_convert_type(lo.astype(BF16).astype(F32), U32)
    hi_bits = lax.bitcast_convert_type(hi.astype(BF16).astype(F32), U32)
    return (lo_bits >> 16) | (hi_bits & jnp.uint32(0xFFFF0000))


def _unpack_bf16_pair(w):
    lo = lax.bitcast_convert_type(w << 16, F32)
    hi = lax.bitcast_convert_type(w & jnp.uint32(0xFFFF0000), F32)
    return lo, hi


def _mm_kernel(a_ref, b_ref, o_ref):
    o_ref[...] = jnp.dot(a_ref[...], b_ref[...], preferred_element_type=F32).astype(o_ref.dtype)


def _matmul(a, b, out_dtype, tm, tn):
    m, k = a.shape
    n = b.shape[1]
    tm, tn = min(tm, m), min(tn, n)
    return pl.pallas_call(
        _mm_kernel,
        grid=(n // tn, m // tm),
        in_specs=[pl.BlockSpec((tm, k), lambda j, i: (i, 0)), pl.BlockSpec((k, tn), lambda j, i: (0, j))],
        out_specs=pl.BlockSpec((tm, tn), lambda j, i: (i, j)),
        out_shape=jax.ShapeDtypeStruct((m, n), out_dtype),
        compiler_params=_cparams(("arbitrary", "arbitrary")),
        name="proj_in",
    )(a, b)


def _rope_pair(t):
    r = t + pltpu.roll(t, 64, axis=1)
    lane = lax.broadcasted_iota(I32, r.shape, 1)
    return jnp.where(lane < ROPE, r, 0.0)


def _mla_prep_kernel(c_ref, cs_ref, qg_ref, kvg_ref, wq_ref, wkt_ref, wv_ref, q_ref, kt_ref, vp_ref, *, scale):
    cq = c_ref[:, 0:Q_RANK]
    ckv = c_ref[:, Q_RANK:Q_RANK + KV_RANK]
    kblk = c_ref[:, Q_RANK + KV_RANK:Q_RANK + KV_RANK + LANES]
    cs = cs_ref[...]
    cqn = (cq * lax.rsqrt(jnp.mean(cq * cq, -1, keepdims=True) + RMS_EPS) * qg_ref[...]).astype(BF16)
    ckvn = (ckv * lax.rsqrt(jnp.mean(ckv * ckv, -1, keepdims=True) + RMS_EPS) * kvg_ref[...]).astype(BF16)
    k_rope_t = jnp.transpose(_rope_pair(kblk * cs)).astype(BF16)
    lane = lax.broadcasted_iota(I32, (cq.shape[0], LANES), 1)
    ones_col = jnp.where(lane == 0, 1.0, 0.0).astype(BF16)
    nt = (((1,), (1,)), ((), ()))
    for h in range(N_HEADS):
        qh = jnp.dot(cqn, wq_ref[:, h * HEAD_W:(h + 1) * HEAD_W], preferred_element_type=F32)
        q_ref[:, h * HEAD_W:h * HEAD_W + NOPE] = (qh[:, :NOPE] * scale).astype(BF16)
        q_ref[:, h * HEAD_W + NOPE:(h + 1) * HEAD_W] = (_rope_pair(qh[:, NOPE:] * cs) * scale).astype(BF16)
        kt = lax.dot_general(wkt_ref[h * NOPE:(h + 1) * NOPE, :], ckvn, nt, preferred_element_type=F32)
        kt_ref[h * HEAD_W:h * HEAD_W + NOPE, :] = kt.astype(BF16)
        kt_ref[h * HEAD_W + NOPE:(h + 1) * HEAD_W, :] = k_rope_t
        vh = jnp.dot(ckvn, wv_ref[:, h * V_DIM:(h + 1) * V_DIM], preferred_element_type=F32)
        vp_ref[:, h * HEAD_W:h * HEAD_W + V_DIM] = vh.astype(BF16)
        vp_ref[:, h * HEAD_W + V_DIM:(h + 1) * HEAD_W] = ones_col


def _mla_prep(proj, cs, qg, kvg, wq, wkt, wv, tm):
    s = proj.shape[0]
    cw = Q_RANK + KV_RANK + LANES
    scale = float((NOPE + ROPE) ** -0.5 * math.log2(math.e))
    hw = N_HEADS * HEAD_W
    return pl.pallas_call(
        functools.partial(_mla_prep_kernel, scale=scale),
        grid=(s // tm,),
        in_specs=[
            pl.BlockSpec((tm, cw), lambda i: (i, 0)),
            pl.BlockSpec((tm, LANES), lambda i: (i, 0)),
            pl.BlockSpec((1, Q_RANK), lambda i: (0, 0)),
            pl.BlockSpec((1, KV_RANK), lambda i: (0, 0)),
            pl.BlockSpec((Q_RANK, hw), lambda i: (0, 0)),
            pl.BlockSpec((N_HEADS * NOPE, KV_RANK), lambda i: (0, 0)),
            pl.BlockSpec((KV_RANK, N_HEADS * V_DIM), lambda i: (0, 0)),
        ],
        out_specs=[
            pl.BlockSpec((tm, hw), lambda i: (i, 0)),
            pl.BlockSpec((None, hw, tm), lambda i: (i, 0, 0)),
            pl.BlockSpec((tm, hw), lambda i: (i, 0)),
        ],
        out_shape=[
            jax.ShapeDtypeStruct((s, hw), BF16),
            jax.ShapeDtypeStruct((s // tm, hw, tm), BF16),
            jax.ShapeDtypeStruct((s, hw), BF16),
        ],
        compiler_params=_cparams(("arbitrary",)),
        name="mla_prep",
    )(proj, cs, qg, kvg, wq, wkt, wv)


def _flash_kernel(q_ref, k_ref, v_ref, o_ref, s_scr, m_scr, acc_scr, *, t, ku):
    i = pl.program_id(1)
    m_scr[...] = jnp.full(m_scr.shape, -jnp.inf, F32)
    acc_scr[...] = jnp.zeros(acc_scr.shape, F32)
    nsub = t // ku
    ncol = ku // LANES

    def scores(j, slot):
        q = q_ref[...]
        for u in range(nsub):
            su = jnp.dot(q, k_ref[j * nsub + u], preferred_element_type=F32)
            for c in range(ncol):
                s_scr[slot, u * ncol + c] = su[:, c * LANES:(c + 1) * LANES]

    def consume(j, slot, masked):
        cols = [s_scr[slot, c] for c in range(nsub * ncol)]
        if masked:
            row = lax.broadcasted_iota(I32, (t, LANES), 0)
            lane = lax.broadcasted_iota(I32, (t, LANES), 1)
            cols = [jnp.where(c * LANES + lane <= row, sc, -jnp.inf) for c, sc in enumerate(cols)]
        m_cur = cols[0]
        for sc in cols[1:]:
            m_cur = jnp.maximum(m_cur, sc)
        m_prev = m_scr[...]
        m_new = jnp.maximum(m_prev, jnp.max(m_cur, axis=-1, keepdims=True))
        alpha = jnp.exp2(m_prev - m_new)
        p = jnp.concatenate([jnp.exp2(sc - m_new).astype(BF16) for sc in cols], axis=1)
        v = v_ref[pl.ds(pl.multiple_of(j * t, t), t), :]
        pv = jnp.dot(p, v, preferred_element_type=F32)
        acc_scr[...] = jnp.concatenate([alpha, alpha], axis=1) * acc_scr[...] + pv
        m_scr[...] = m_new

    scores(0, 0)

    def pair(jj, carry):
        scores(2 * jj + 1, 1)
        consume(2 * jj, 0, False)
        scores(2 * jj + 2, 0)
        consume(2 * jj + 1, 1, False)
        return carry

    lax.fori_loop(0, i // 2, pair, 0)

    @pl.when(i % 2 == 1)
    def _():
        scores(i, 1)
        consume(i - 1, 0, False)
        consume(i, 1, True)

    @pl.when(i % 2 == 0)
    def _():
        consume(i, 0, True)

    acc = acc_scr[...]
    o_ref[...] = acc[:, :V_DIM] / acc[:, V_DIM:V_DIM + 1]


def _flash(q, kt, vp, t=512):
    s = q.shape[0]
    ku = kt.shape[2]
    t = min(t, s)
    return pl.pallas_call(
        functools.partial(_flash_kernel, t=t, ku=ku),
        grid=(N_HEADS, s // t),
        in_specs=[
            pl.BlockSpec((t, HEAD_W), lambda h, i: (i, h)),
            pl.BlockSpec((s // ku, HEAD_W, ku), lambda h, i: (0, h, 0)),
            pl.BlockSpec((s, HEAD_W), lambda h, i: (0, h)),
        ],
        out_specs=pl.BlockSpec((t, V_DIM), lambda h, i: (i, h)),
        out_shape=jax.ShapeDtypeStruct((s, N_HEADS * V_DIM), F32),
        scratch_shapes=[pltpu.VMEM((2, t // LANES, t, LANES), F32), pltpu.VMEM((t, LANES), F32),
                        pltpu.VMEM((t, HEAD_W), F32)],
        compiler_params=_cparams(("arbitrary", "arbitrary")),
        name="flash_attn",
    )(q, kt, vp)


def _s5_kernel(u_ref, kc_ref, win_ref, wout_ref, al_ref, y_ref, t_scr, ucat_scr, sc_scr, h_scr, *, nct):
    @pl.when(pl.program_id(1) == 0)
    def _():
        t_scr[...] = jnp.zeros(t_scr.shape, BF16)
        for s in range(SSM_L):
            for s2 in range(s, SSM_L):
                t_scr[s * LANES:(s + 1) * LANES, s2 * LANES:(s2 + 1) * LANES] = kc_ref[s2 - s]
        h_scr[...] = jnp.zeros(h_scr.shape, F32)

    for s in range(SSM_L):
        ucat_scr[:, s * LANES:(s + 1) * LANES] = u_ref[pl.ds(s, nct, stride=SSM_L), :].astype(BF16)
    ucat = ucat_scr[...]
    sc_scr[...] = jnp.dot(ucat, win_ref[...], preferred_element_type=F32)
    a_re = al_ref[:, :SLAB_STATE]
    a_im = al_ref[:, SLAB_STATE:]

    def step(c, carry):
        h_re, h_im = carry
        row = sc_scr[pl.ds(c, 1), :]
        sc_scr[pl.ds(c, 1), :] = jnp.concatenate([h_re, h_im], axis=1)
        n_re = a_re * h_re - a_im * h_im + row[:, :SLAB_STATE]
        n_im = a_re * h_im + a_im * h_re + row[:, SLAB_STATE:]
        return n_re, n_im

    h0 = h_scr[...]
    h_re, h_im = lax.fori_loop(0, nct, step, (h0[:, :SLAB_STATE], h0[:, SLAB_STATE:]))
    h_scr[...] = jnp.concatenate([h_re, h_im], axis=1)
    y = (jnp.dot(ucat, t_scr[...], preferred_element_type=F32)
         + jnp.dot(sc_scr[...].astype(BF16), wout_ref[...], preferred_element_type=F32))
    for s in range(SSM_L):
        y_ref[pl.ds(s, nct, stride=SSM_L), :] = y[:, s * LANES:(s + 1) * LANES]


def _s5(proj, u_col_block, kc, win, wout, al, ts=8192):
    s = proj.shape[0]
    ts = min(ts, s)
    nct = ts // SSM_L
    n_slab = kc.shape[0]
    lw = SSM_L * LANES
    return pl.pallas_call(
        functools.partial(_s5_kernel, nct=nct),
        grid=(n_slab, s // ts),
        in_specs=[
            pl.BlockSpec((ts, LANES), lambda j, t: (t, u_col_block + j)),
            pl.BlockSpec((None, SSM_L, LANES, LANES), lambda j, t: (j, 0, 0, 0)),
            pl.BlockSpec((None, lw, 2 * SLAB_STATE), lambda j, t: (j, 0, 0)),
            pl.BlockSpec((None, 2 * SLAB_STATE, lw), lambda j, t: (j, 0, 0)),
            pl.BlockSpec((None, 1, 2 * SLAB_STATE), lambda j, t: (j, 0, 0)),
        ],
        out_specs=pl.BlockSpec((ts, LANES), lambda j, t: (t, j)),
        out_shape=jax.ShapeDtypeStruct((s, n_slab * LANES), F32),
        scratch_shapes=[
            pltpu.VMEM((lw, lw), BF16),
            pltpu.VMEM((nct, lw), BF16),
            pltpu.VMEM((nct, 2 * SLAB_STATE), F32),
            pltpu.VMEM((1, 2 * SLAB_STATE), F32),
        ],
        compiler_params=_cparams(("arbitrary", "arbitrary")),
        name="s5",
    )(proj, kc, win, wout, al)


def _s5_params(lam_re, lam_im, log_dt, b_re, b_im, c_re, c_im, d_skip):
    g, p = lam_re.shape
    h = b_re.shape[-1]
    n_slab = g // SLAB_G
    hp = lax.Precision.HIGHEST
    lr = jnp.minimum(lam_re.astype(F32), -1e-4)
    li = lam_im.astype(F32)
    dt = jnp.exp(log_dt.astype(F32))[:, None]
    kk = jnp.arange(SSM_L + 1, dtype=F32)[:, None, None]
    mag = jnp.exp(lr * dt * kk)
    ang = li * dt * kk
    pw_re, pw_im = mag * jnp.cos(ang), mag * jnp.sin(ang)
    x, y = pw_re[1] - 1.0, pw_im[1]
    den = lr * lr + li * li
    f_re, f_im = (x * lr + y * li) / den, (y * lr - x * li) / den
    bb_re = f_re[..., None] * b_re - f_im[..., None] * b_im
    bb_im = f_re[..., None] * b_im + f_im[..., None] * b_re
    cp_re = c_re[None] * pw_re[:SSM_L, :, None, :] - c_im[None] * pw_im[:SSM_L, :, None, :]
    cp_im = c_re[None] * pw_im[:SSM_L, :, None, :] + c_im[None] * pw_re[:SSM_L, :, None, :]
    kmat = (jnp.einsum('kgop,gpi->gkoi', cp_re, bb_re, precision=hp)
            - jnp.einsum('kgop,gpi->gkoi', cp_im, bb_im, precision=hp))
    kmat = kmat.at[:, 0].add(jnp.eye(h, dtype=F32)[None] * d_skip.astype(F32)[:, :, None])

    def same_group(rows, row_div, cols, col_div):
        r = (jnp.arange(rows, dtype=I32) // row_div) % SLAB_G
        c = (jnp.arange(cols, dtype=I32) // col_div) % SLAB_G
        return r[:, None] == c[None, :]

    km = kmat.reshape(n_slab, SLAB_G, SSM_L, h, h).transpose(0, 2, 1, 4, 3)
    km = jnp.broadcast_to(km.reshape(n_slab, SSM_L, LANES, 1, h), (n_slab, SSM_L, LANES, SLAB_G, h))
    kc = jnp.where(same_group(LANES, h, LANES, h), km.reshape(n_slab, SSM_L, LANES, LANES), 0.0).astype(BF16)
    rk = (SSM_L - 1) - jnp.arange(SSM_L, dtype=F32)[:, None, None]
    rmag, rang = jnp.exp(lr * dt * rk), li * dt * rk
    rp_re, rp_im = rmag * jnp.cos(rang), rmag * jnp.sin(rang)
    wi_re = rp_re[:, :, None, :] * bb_re.transpose(0, 2, 1)[None] - rp_im[:, :, None, :] * bb_im.transpose(0, 2, 1)[None]
    wi_im = rp_re[:, :, None, :] * bb_im.transpose(0, 2, 1)[None] + rp_im[:, :, None, :] * bb_re.transpose(0, 2, 1)[None]

    def slab_in(w):
        w = w.reshape(SSM_L, n_slab, SLAB_G, h, p).transpose(1, 0, 2, 3, 4).reshape(n_slab, SSM_L * LANES, 1, p)
        w = jnp.broadcast_to(w, (n_slab, SSM_L * LANES, SLAB_G, p)).reshape(n_slab, SSM_L * LANES, SLAB_STATE)
        return jnp.where(same_group(SSM_L * LANES, h, SLAB_STATE, p), w, 0.0)

    win = jnp.concatenate([slab_in(wi_re), slab_in(wi_im)], axis=-1).astype(BF16)
    q_re, q_im = pw_re[1:SSM_L + 1], pw_im[1:SSM_L + 1]
    cl_re = c_re[None] * q_re[:, :, None, :] - c_im[None] * q_im[:, :, None, :]
    cl_im = c_re[None] * q_im[:, :, None, :] + c_im[None] * q_re[:, :, None, :]

    def slab_out(w):
        w = w.reshape(SSM_L, n_slab, SLAB_G, h, p).transpose(1, 2, 4, 0, 3).reshape(n_slab, SLAB_STATE, SSM_L, 1, h)
        w = jnp.broadcast_to(w, (n_slab, SLAB_STATE, SSM_L, SLAB_G, h)).reshape(n_slab, SLAB_STATE, SSM_L * LANES)
        return jnp.where(same_group(SLAB_STATE, p, SSM_L * LANES, h), w, 0.0)

    wout = jnp.concatenate([slab_out(cl_re), slab_out(-cl_im)], axis=1).astype(BF16)
    al = jnp.concatenate([pw_re[SSM_L].reshape(n_slab, 1, SLAB_STATE),
                          pw_im[SSM_L].reshape(n_slab, 1, SLAB_STATE)], axis=-1)
    return kc, win, wout, al


def _heads_kernel(attn_ref, y_ref, ag_ref, sg_ref, wglu_ref, bglu_ref, o_ref):
    a = attn_ref[...]
    da = a.shape[1]
    o_ref[:, :da] = (a * lax.rsqrt(jnp.mean(a * a, -1, keepdims=True) + RMS_EPS) * ag_ref[...]).astype(BF16)
    g = jax.nn.gelu(y_ref[...])
    z = jnp.dot(g.astype(BF16), wglu_ref[...], preferred_element_type=F32) + bglu_ref[...]
    ssm = g * _sigmoid(z)
    o_ref[:, da:] = (ssm * lax.rsqrt(jnp.mean(ssm * ssm, -1, keepdims=True) + RMS_EPS) * sg_ref[...]).astype(BF16)


def _heads(attn, y, ag, sg, wglu, bglu, tm=512):
    s, da = attn.shape
    ds = y.shape[1]
    tm = min(tm, s)
    return pl.pallas_call(
        _heads_kernel,
        grid=(s // tm,),
        in_specs=[
            pl.BlockSpec((tm, da), lambda i: (i, 0)),
            pl.BlockSpec((tm, ds), lambda i: (i, 0)),
            pl.BlockSpec((1, da), lambda i: (0, 0)),
            pl.BlockSpec((1, ds), lambda i: (0, 0)),
            pl.BlockSpec((ds, ds), lambda i: (0, 0)),
            pl.BlockSpec((1, ds), lambda i: (0, 0)),
        ],
        out_specs=pl.BlockSpec((tm, da + ds), lambda i: (i, 0)),
        out_shape=jax.ShapeDtypeStruct((s, da + ds), BF16),
        compiler_params=_cparams(("arbitrary",)),
        name="heads",
    )(attn, y, ag, sg, wglu, bglu)


def _layernorm_rows(x, g, b):
    mu = jnp.mean(x, -1, keepdims=True)
    xc = x - mu
    var = jnp.mean(xc * xc, -1, keepdims=True)
    return xc * lax.rsqrt(var + LN_EPS) * g + b


def _outproj_kernel(h_ref, w_ref, x_ref, g_ref, b_ref, o_ref, p_ref, *, alpha, tn, nj):
    j = pl.program_id(1)
    pre = alpha * x_ref[...] + jnp.dot(h_ref[...], w_ref[...], preferred_element_type=F32)
    for jj in range(nj):
        @pl.when(j == jj)
        def _(jj=jj):
            o_ref[:, jj * tn:(jj + 1) * tn] = pre

    @pl.when(j == nj - 1)
    def _():
        xn = _layernorm_rows(o_ref[...], g_ref[...], b_ref[...])
        o_ref[...] = xn
        half = xn.shape[1] // 2
        p_ref[...] = _pack_bf16_pair(xn[:, :half], xn[:, half:])


def _outproj_ln(heads, w, x, g, b, alpha, tm=512, tn=512):
    s, k = heads.shape
    n = w.shape[1]
    tm, tn = min(tm, s), min(tn, n)
    nj = n // tn
    return pl.pallas_call(
        functools.partial(_outproj_kernel, alpha=alpha, tn=tn, nj=nj),
        grid=(s // tm, nj),
        in_specs=[
            pl.BlockSpec((tm, k), lambda i, j: (i, 0)),
            pl.BlockSpec((k, tn), lambda i, j: (0, j)),
            pl.BlockSpec((tm, tn), lambda i, j: (i, j)),
            pl.BlockSpec((1, n), lambda i, j: (0, 0)),
            pl.BlockSpec((1, n), lambda i, j: (0, 0)),
        ],
        out_specs=[pl.BlockSpec((tm, n), lambda i, j: (i, 0)), pl.BlockSpec((tm, n // 2), lambda i, j: (i, 0))],
        out_shape=[jax.ShapeDtypeStruct((s, n), F32), jax.ShapeDtypeStruct((s, n // 2), U32)],
        compiler_params=_cparams(("arbitrary", "arbitrary")),
        name="outproj_ln",
    )(heads, w, x, g, b)


def _first_argmax(v, iota, n):
    m = jnp.max(v, axis=0, keepdims=True)
    first = jnp.min(jnp.where(v == m, iota, n), axis=0, keepdims=True)
    return m, first


def _router_kernel(x_ref, wh_ref, wl_ref, bias_ref, tri_ref, e_ref, g_ref, r_ref, cnt_ref, carry_scr):
    @pl.when(pl.program_id(0) == 0)
    def _():
        carry_scr[...] = jnp.zeros(carry_scr.shape, F32)

    x = x_ref[...]
    xh = x.astype(BF16)
    xl = (x - xh.astype(F32)).astype(BF16)
    dn = (((1,), (1,)), ((), ()))
    wh, wl = wh_ref[...], wl_ref[...]
    logits = (lax.dot_general(wh, xh, dn, preferred_element_type=F32)
              + lax.dot_general(wh, xl, dn, preferred_element_type=F32)
              + lax.dot_general(wl, xh, dn, preferred_element_type=F32))
    scores = _sigmoid(logits)
    sel = scores + bias_ref[...]
    t = sel.shape[1]
    per = N_EXPERTS // N_GROUPS
    ninf = -jnp.inf
    iota_g = lax.broadcasted_iota(I32, (per, t), 0)
    gs_rows = []
    for gi in range(N_GROUPS):
        blk = sel[gi * per:(gi + 1) * per, :]
        m1, f1 = _first_argmax(blk, iota_g, per)
        m2 = jnp.max(jnp.where(iota_g == f1, ninf, blk), axis=0, keepdims=True)
        gs_rows.append(m1 + m2)
    gs = jnp.concatenate(gs_rows, axis=0)
    iota_ng = lax.broadcasted_iota(I32, (N_GROUPS, t), 0)
    gmask = jnp.zeros((N_GROUPS, t), I32)
    for _ in range(TOPK_GROUPS):
        _, f = _first_argmax(gs, iota_ng, N_GROUPS)
        pick = iota_ng == f
        gmask = jnp.where(pick, 1, gmask)
        gs = jnp.where(pick, ninf, gs)
    selm = jnp.concatenate(
        [jnp.where(gmask[gi:gi + 1, :] > 0, sel[gi * per:(gi + 1) * per, :], ninf) for gi in range(N_GROUPS)], axis=0)
    iota_e = lax.broadcasted_iota(I32, (N_EXPERTS, t), 0)
    onehot = jnp.zeros((N_EXPERTS, t), F32)
    e_rows, g_rows = [], []
    for _ in range(TOP_K):
        _, f = _first_argmax(selm, iota_e, N_EXPERTS)
        pick = iota_e == f
        e_rows.append(f)
        g_rows.append(jnp.sum(jnp.where(pick, scores, 0.0), axis=0, keepdims=True))
        onehot = jnp.where(pick, 1.0, onehot)
        selm = jnp.where(pick, ninf, selm)
    gate = jnp.concatenate(g_rows, axis=0)
    gate = gate / jnp.sum(gate, axis=0, keepdims=True) * ROUTED_SCALE
    cum = jnp.dot(onehot.astype(BF16), tri_ref[...], preferred_element_type=F32)
    rank_e = cum - onehot + carry_scr[:, 0:1]
    r_rows = [jnp.sum(jnp.where(iota_e == f, rank_e, 0.0), axis=0, keepdims=True) for f in e_rows]
    carry = carry_scr[...] + cum[:, t - 1:t]
    carry_scr[...] = carry
    e_ref[...] = jnp.concatenate(e_rows, axis=0)
    g_ref[...] = gate
    r_ref[...] = jnp.concatenate(r_rows, axis=0).astype(I32)
    cnt_ref[...] = carry


def _router(x, wh, wl, bias, tm=512):
    t, d = x.shape
    tm = min(tm, t)
    tri = (jnp.arange(tm)[:, None] <= jnp.arange(tm)[None, :]).astype(BF16)
    return pl.pallas_call(
        _router_kernel,
        grid=(t // tm,),
        in_specs=[
            pl.BlockSpec((tm, d), lambda i: (i, 0)),
            pl.BlockSpec((N_EXPERTS, d), lambda i: (0, 0)),
            pl.BlockSpec((N_EXPERTS, d), lambda i: (0, 0)),
            pl.BlockSpec((N_EXPERTS, 1), lambda i: (0, 0)),
            pl.BlockSpec((tm, tm), lambda i: (0, 0)),
        ],
        out_specs=[
            pl.BlockSpec((TOP_K, tm), lambda i: (0, i)),
            pl.BlockSpec((TOP_K, tm), lambda i: (0, i)),
            pl.BlockSpec((TOP_K, tm), lambda i: (0, i)),
            pl.BlockSpec((N_EXPERTS, LANES), lambda i: (0, 0)),
        ],
        out_shape=[
            jax.ShapeDtypeStruct((TOP_K, t), I32),
            jax.ShapeDtypeStruct((TOP_K, t), F32),
            jax.ShapeDtypeStruct((TOP_K, t), I32),
            jax.ShapeDtypeStruct((N_EXPERTS, LANES), F32),
        ],
        scratch_shapes=[pltpu.VMEM((N_EXPERTS, LANES), F32)],
        compiler_params=_cparams(("arbitrary",)),
        name="router",
    )(x, wh, wl, bias, tri)


def _dispatch_kernel(pos_ref, x_ref, xs_hbm, sem, *, tt):
    def issue(t, carry):
        for k in range(TOP_K):
            pltpu.make_async_copy(x_ref.at[pl.ds(t, 1)], xs_hbm.at[pl.ds(pos_ref[k, t], 1)], sem).start(
                priority=k % 2)
        return carry

    lax.fori_loop(0, tt, issue, 0)
    for k in range(TOP_K):
        pltpu.make_async_copy(x_ref, xs_hbm.at[pl.ds(0, tt)], sem).wait()


def _dispatch(pos, xp, tt=512):
    t, w = xp.shape
    tt = min(tt, t)
    return pl.pallas_call(
        functools.partial(_dispatch_kernel, tt=tt),
        grid=(t // tt,),
        in_specs=[
            pl.BlockSpec((TOP_K, tt), lambda i: (0, i), memory_space=pltpu.SMEM),
            pl.BlockSpec((tt, w), lambda i: (i, 0)),
        ],
        out_specs=pl.BlockSpec(memory_space=pl.ANY),
        out_shape=jax.ShapeDtypeStruct((t * TOP_K, w), U32),
        scratch_shapes=[pltpu.SemaphoreType.DMA(())],
        compiler_params=_cparams(("arbitrary",)),
        name="dispatch",
    )(pos, xp)


def _experts_kernel(vblk_ref, vexp_ref, vlo_ref, vhi_ref, vfirst_ref, vnew_ref, xs_ref, wg_ref, wu_ref, wdn_ref,
                    ys_ref, wgu_scr, wd_scr, *, bm):
    v = pl.program_id(0)
    lo, hi = vlo_ref[v], vhi_ref[v]

    @pl.when(vnew_ref[v] == 1)
    def _():
        wgu_scr[:, :D_FF] = wg_ref[...].astype(BF16)
        wgu_scr[:, D_FF:] = wu_ref[...].astype(BF16)
        wd_scr[...] = wdn_ref[...].astype(BF16)

    @pl.when(hi > lo)
    def _():
        xa, xb = _unpack_bf16_pair(xs_ref[...])
        half = xa.shape[1]
        gu = (jnp.dot(xa.astype(BF16), wgu_scr[:half, :], preferred_element_type=F32)
              + jnp.dot(xb.astype(BF16), wgu_scr[half:, :], preferred_element_type=F32))
        hmid = (gu[:, :D_FF] * _sigmoid(gu[:, :D_FF])) * gu[:, D_FF:]
        y = jnp.dot(hmid.astype(BF16), wd_scr[...], preferred_element_type=F32)
        packed = _pack_bf16_pair(y[:, :half], y[:, half:])
        rows = vblk_ref[v] * bm + lax.broadcasted_iota(I32, packed.shape, 0)
        mine = (rows >= lo) & (rows < hi)

        @pl.when(vfirst_ref[v] == 1)
        def _():
            ys_ref[...] = jnp.where(mine, packed, jnp.uint32(0))

        @pl.when(vfirst_ref[v] == 0)
        def _():
            ys_ref[...] = jnp.where(mine, packed, ys_ref[...])


def _experts(sched, xs, w_gate, w_up, w_down, layer, bm):
    r, w = xs.shape
    d = w_down.shape[3]
    nv = sched[0].shape[0]
    grid_spec = pltpu.PrefetchScalarGridSpec(
        num_scalar_prefetch=6,
        grid=(nv,),
        in_specs=[
            pl.BlockSpec((bm, w), lambda v, vb, ve, lo, hi, fi, nw: (vb[v], 0)),
            pl.BlockSpec((None, None, d, D_FF), lambda v, vb, ve, lo, hi, fi, nw: (layer, ve[v], 0, 0)),
            pl.BlockSpec((None, None, d, D_FF), lambda v, vb, ve, lo, hi, fi, nw: (layer, ve[v], 0, 0)),
            pl.BlockSpec((None, None, D_FF, d), lambda v, vb, ve, lo, hi, fi, nw: (layer, ve[v], 0, 0)),
        ],
        out_specs=pl.BlockSpec((bm, w), lambda v, vb, ve, lo, hi, fi, nw: (vb[v], 0)),
        scratch_shapes=[pltpu.VMEM((d, 2 * D_FF), BF16), pltpu.VMEM((D_FF, d), BF16)],
    )
    return pl.pallas_call(
        functools.partial(_experts_kernel, bm=bm),
        grid_spec=grid_spec,
        out_shape=jax.ShapeDtypeStruct((r, w), U32),
        compiler_params=_cparams(("arbitrary",)),
        name="experts",
    )(*sched, xs, w_gate, w_up, w_down)


def _expert_schedule(counts, n_rows, bm):
    counts = counts.astype(I32)
    ends = jnp.cumsum(counts)
    starts = ends - counts
    nb = n_rows // bm
    nv = nb + N_EXPERTS - 1
    first_blk = starts // bm
    last_blk = jnp.maximum(ends - 1, 0) // bm
    nvis = jnp.where(counts > 0, last_blk - first_blk + 1, 0)
    vis_end = jnp.cumsum(nvis)
    vis_start = vis_end - nvis
    total = vis_end[-1]
    v = jnp.arange(nv, dtype=I32)
    e = jnp.minimum(jnp.sum((vis_end[None, :] <= v[:, None]).astype(I32), axis=1), N_EXPERTS - 1)
    blk = first_blk[e] + (v - vis_start[e])
    real = v < total
    last_v = jnp.maximum(total - 1, 0)
    e = jnp.where(real, e, e[last_v])
    blk = jnp.where(real, blk, blk[last_v]).astype(I32)
    lo = jnp.where(real, starts[e], 0).astype(I32)
    hi = jnp.where(real, ends[e], 0).astype(I32)
    prev_blk = jnp.concatenate([jnp.full((1,), -1, I32), blk[:-1]])
    first = (real & (blk != prev_blk)).astype(I32)
    prev_e = jnp.concatenate([jnp.full((1,), -1, I32), e[:-1]])
    new_e = (e != prev_e).astype(I32)
    return starts, (blk, e, lo, hi, first, new_e)


def _combine_kernel(pos_ref, ys_hbm, gate_ref, x_ref, wsgu_ref, wsd_ref, g_ref, b_ref, o_ref, ob_ref,
                    buf, sem, *, tc, alpha):
    def issue(t, carry):
        for k in range(TOP_K):
            pltpu.make_async_copy(ys_hbm.at[pl.ds(pos_ref[k, t], 1)], buf.at[k, pl.ds(t, 1)], sem).start(
                priority=k % 2)
        return carry

    lax.fori_loop(0, tc, issue, 0)
    x = x_ref[...]
    xb = x.astype(BF16)
    gu = jnp.dot(xb, wsgu_ref[...], preferred_element_type=F32)
    hmid = (gu[:, :D_FF] * _sigmoid(gu[:, :D_FF])) * gu[:, D_FF:]
    shared = jnp.dot(hmid.astype(BF16), wsd_ref[...], preferred_element_type=F32)
    for k in range(TOP_K):
        pltpu.make_async_copy(ys_hbm.at[pl.ds(0, tc)], buf.at[k], sem).wait()
    half = x.shape[1] // 2
    acc_lo = shared[:, :half]
    acc_hi = shared[:, half:]
    gate = gate_ref[...]
    for k in range(TOP_K):
        lo, hi = _unpack_bf16_pair(buf[k])
        gk = gate[:, k:k + 1]
        acc_lo = acc_lo + gk * lo
        acc_hi = acc_hi + gk * hi
    pre = alpha * x + jnp.concatenate([acc_lo, acc_hi], axis=1)
    xn = _layernorm_rows(pre, g_ref[...], b_ref[...])
    o_ref[...] = xn
    ob_ref[...] = xn.astype(BF16)


def _combine_ln(pos, ys, gate_tok, x, wsgu, wsd, g, b, alpha, tc=256):
    t, d = x.shape
    tc = min(tc, t)
    w = ys.shape[1]
    return pl.pallas_call(
        functools.partial(_combine_kernel, tc=tc, alpha=alpha),
        grid=(t // tc,),
        in_specs=[
            pl.BlockSpec((TOP_K, tc), lambda i: (0, i), memory_space=pltpu.SMEM),
            pl.BlockSpec(memory_space=pl.ANY),
            pl.BlockSpec((tc, TOP_K), lambda i: (i, 0)),
            pl.BlockSpec((tc, d), lambda i: (i, 0)),
            pl.BlockSpec((d, 2 * D_FF), lambda i: (0, 0)),
            pl.BlockSpec((D_FF, d), lambda i: (0, 0)),
            pl.BlockSpec((1, d), lambda i: (0, 0)),
            pl.BlockSpec((1, d), lambda i: (0, 0)),
        ],
        out_specs=[pl.BlockSpec((tc, d), lambda i: (i, 0)), pl.BlockSpec((tc, d), lambda i: (i, 0))],
        out_shape=[jax.ShapeDtypeStruct((t, d), F32), jax.ShapeDtypeStruct((t, d), BF16)],
        scratch_shapes=[pltpu.VMEM((TOP_K, tc, w), U32), pltpu.SemaphoreType.DMA(())],
        compiler_params=_cparams(("arbitrary",)),
        name="combine_ln",
    )(pos, ys, gate_tok, x, wsgu, wsd, g, b)


def _swap_rope_cols(w):
    half = w.shape[-1] // 2
    return jnp.concatenate([-w[..., half:], w[..., :half]], axis=-1)


def _prep_w_in(w_in):
    off_kr = Q_RANK + KV_RANK
    w_kr = w_in[:, off_kr:off_kr + ROPE]
    return jnp.concatenate(
        [w_in[:, :off_kr], w_kr, _swap_rope_cols(w_kr), w_in[:, off_kr + ROPE:]], axis=1).astype(BF16)


def _prep_w_uq(w_uq):
    w = w_uq.reshape(Q_RANK, N_HEADS, NOPE + ROPE)
    w_r = w[..., NOPE:]
    return jnp.concatenate([w[..., :NOPE], w_r, _swap_rope_cols(w_r)], axis=-1).reshape(
        Q_RANK, N_HEADS * HEAD_W).astype(BF16)


def kernel(x, positions, w_in, q_norm_g, kv_norm_g, w_uq, w_ukv, ssm_lambda_re, ssm_lambda_im, ssm_log_dt,
           ssm_b_re, ssm_b_im, ssm_c_re, ssm_c_im, ssm_d, w_glu, b_glu, attn_out_norm_g, ssm_out_norm_g,
           w_out, ln1_g, ln1_b, w_router, router_bias, w_gate, w_up, w_down, ws_gate, ws_up, ws_down,
           ln2_g, ln2_b):
    batch, seq, d_model = x.shape
    assert batch == 1
    depth = w_in.shape[0]
    alpha = float((2 * depth) ** 0.25)
    u_col_block = (Q_RANK + KV_RANK + LANES) // LANES

    inv_freq = ROPE_THETA ** (-(jnp.arange(0, ROPE, 2, dtype=F32) / ROPE))
    ang = positions.astype(F32)[0][:, None] * inv_freq
    cos, sin = jnp.cos(ang), jnp.sin(ang)
    cs = jnp.concatenate([cos, cos, sin, sin], axis=1)

    xf = x[0]
    xb = xf.astype(BF16)
    for l in range(depth):
        proj = _matmul(xb, _prep_w_in(w_in[l]), F32, tm=512, tn=1152)
        w_kv = w_ukv[l].reshape(KV_RANK, N_HEADS, NOPE + V_DIM)
        wkt = w_kv[..., :NOPE].reshape(KV_RANK, N_HEADS * NOPE).T.astype(BF16)
        wv = w_kv[..., NOPE:].reshape(KV_RANK, N_HEADS * V_DIM).astype(BF16)
        q, kt, vp = _mla_prep(proj, cs, q_norm_g[l][None], kv_norm_g[l][None],
                              _prep_w_uq(w_uq[l]), wkt, wv, min(MLA_TM, seq))
        attn = _flash(q, kt, vp)
        kc, win, wout, al = _s5_params(ssm_lambda_re[l], ssm_lambda_im[l], ssm_log_dt[l], ssm_b_re[l],
                                       ssm_b_im[l], ssm_c_re[l], ssm_c_im[l], ssm_d[l])
        y = _s5(proj, u_col_block, kc, win, wout, al)
        heads = _heads(attn, y, attn_out_norm_g[l][None], ssm_out_norm_g[l][None],
                       w_glu[l].astype(BF16), b_glu[l][None])
        x1, x1p = _outproj_ln(heads, w_out[l].astype(BF16), xf, ln1_g[l][None], ln1_b[l][None], alpha)
        wr_t = w_router[l].T
        wr_h = wr_t.astype(BF16)
        wr_l = (wr_t - wr_h.astype(F32)).astype(BF16)
        top_e, gate, rank, cnt = _router(x1, wr_h, wr_l, router_bias[l][:, None])
        starts, sched = _expert_schedule(cnt[:, 0], seq * TOP_K, min(MOE_BM, seq * TOP_K))
        e_ids = jnp.arange(N_EXPERTS, dtype=I32)[:, None, None]
        pos = rank + jnp.sum(jnp.where(top_e[None] == e_ids, starts[:, None, None], 0), axis=0)
        xs = _dispatch(pos, x1p)
        ys = _experts(sched, xs, w_gate, w_up, w_down, l, min(MOE_BM, seq * TOP_K))
        wsgu = jnp.concatenate([ws_gate[l], ws_up[l]], axis=-1).astype(BF16)
        xf, xb = _combine_ln(pos, ys, gate.T, x1, wsgu, ws_down[l].astype(BF16),
                             ln2_g[l][None], ln2_b[l][None], alpha)
    return xf[None]
```

```python
import functools
import math

import jax
import jax.numpy as jnp
from jax import lax
from jax.experimental import pallas as pl
from jax.experimental.pallas import tpu as pltpu

F32 = jnp.float32
BF16 = jnp.bfloat16
I32 = jnp.int32
U32 = jnp.uint32

N_HEADS = 16
NOPE = 128
ROPE = 64
V_DIM = 128
HEAD_W = 256
Q_RANK = 768
KV_RANK = 512
ROPE_THETA = 10000.0
SSM_H = 16
SSM_P = 64
N_EXPERTS = 64
D_FF = 256
TOP_K = 8
N_GROUPS = 8
TOPK_GROUPS = 4
ROUTED_SCALE = 2.5
LN_EPS = 1e-5
RMS_EPS = 1e-6

LANES = 128
V7X_VMEM_LIMIT = 56 * 1024 * 1024
SSM_L = 16
SLAB_G = LANES // SSM_H
SLAB_STATE = SLAB_G * SSM_P
MOE_BM = 256
MLA_TM = 256


def _cparams(sem):
    return pltpu.CompilerParams(dimension_semantics=sem, vmem_limit_bytes=V7X_VMEM_LIMIT)


def _sigmoid(z):
    return 1.0 / (1.0 + jnp.exp(-z))


def _pack_bf16_pair(lo, hi):
    lo_bits = lax.bitcast_convert_type(lo.astype(BF16).astype(F32), U32)
    hi_bits = lax.bitcast_convert_type(hi.astype(BF16).astype(F32), U32)
    return (lo_bits >> 16) | (hi_bits & jnp.uint32(0xFFFF0000))


def _unpack_bf16_pair(w):
    lo = lax.bitcast_convert_type(w << 16, F32)
    hi = lax.bitcast_convert_type(w & jnp.uint32(0xFFFF0000), F32)
    return lo, hi


def _mm_kernel(a_ref, b_ref, o_ref):
    o_ref[...] = jnp.dot(a_ref[...].astype(BF16), b_ref[...], preferred_element_type=F32).astype(o_ref.dtype)


def _matmul(a, b, out_dtype, tm, tn):
    m, k = a.shape
    n = b.shape[1]
    tm, tn = min(tm, m), min(tn, n)
    return pl.pallas_call(
        _mm_kernel,
        grid=(n // tn, m // tm),
        in_specs=[pl.BlockSpec((tm, k), lambda j, i: (i, 0)), pl.BlockSpec((k, tn), lambda j, i: (0, j))],
        out_specs=pl.BlockSpec((tm, tn), lambda j, i: (i, j)),
        out_shape=jax.ShapeDtypeStruct((m, n), out_dtype),
        compiler_params=_cparams(("arbitrary", "arbitrary")),
        name="proj_in",
    )(a, b)


def _rope_pair(t):
    r = t + pltpu.roll(t, 64, axis=1)
    lane = lax.broadcasted_iota(I32, r.shape, 1)
    return jnp.where(lane < ROPE, r, 0.0)


def _mla_prep_kernel(c_ref, cs_ref, qg_ref, kvg_ref, wq_ref, wkt_ref, wv_ref, q_ref, kt_ref, vp_ref, *, scale):
    cq = c_ref[:, 0:Q_RANK]
    ckv = c_ref[:, Q_RANK:Q_RANK + KV_RANK]
    kblk = c_ref[:, Q_RANK + KV_RANK:Q_RANK + KV_RANK + LANES]
    cs = cs_ref[...]
    cqn = (cq * lax.rsqrt(jnp.mean(cq * cq, -1, keepdims=True) + RMS_EPS) * qg_ref[...]).astype(BF16)
    ckvn = (ckv * lax.rsqrt(jnp.mean(ckv * ckv, -1, keepdims=True) + RMS_EPS) * kvg_ref[...]).astype(BF16)
    k_rope_t = jnp.transpose(_rope_pair(kblk * cs)).astype(BF16)
    lane = lax.broadcasted_iota(I32, (cq.shape[0], LANES), 1)
    ones_col = jnp.where(lane == 0, 1.0, 0.0).astype(BF16)
    nt = (((1,), (1,)), ((), ()))
    for h in range(N_HEADS):
        qh = jnp.dot(cqn, wq_ref[:, h * HEAD_W:(h + 1) * HEAD_W], preferred_element_type=F32)
        q_ref[:, h * HEAD_W:h * HEAD_W + NOPE] = (qh[:, :NOPE] * scale).astype(BF16)
        q_ref[:, h * HEAD_W + NOPE:(h + 1) * HEAD_W] = (_rope_pair(qh[:, NOPE:] * cs) * scale).astype(BF16)
        kt = lax.dot_general(wkt_ref[h * NOPE:(h + 1) * NOPE, :], ckvn, nt, preferred_element_type=F32)
        kt_ref[h * HEAD_W:h * HEAD_W + NOPE, :] = kt.astype(BF16)
        kt_ref[h * HEAD_W + NOPE:(h + 1) * HEAD_W, :] = k_rope_t
        vh = jnp.dot(ckvn, wv_ref[:, h * V_DIM:(h + 1) * V_DIM], preferred_element_type=F32)
        vp_ref[:, h * HEAD_W:h * HEAD_W + V_DIM] = vh.astype(BF16)
        vp_ref[:, h * HEAD_W + V_DIM:(h + 1) * HEAD_W] = ones_col


def _mla_prep(proj, cs, qg, kvg, wq, wkt, wv, tm):
    s = proj.shape[0]
    cw = Q_RANK + KV_RANK + LANES
    scale = float((NOPE + ROPE) ** -0.5 * math.log2(math.e))
    hw = N_HEADS * HEAD_W
    return pl.pallas_call(
        functools.partial(_mla_prep_kernel, scale=scale),
        grid=(s // tm,),
        in_specs=[
            pl.BlockSpec((tm, cw), lambda i: (i, 0)),
            pl.BlockSpec((tm, LANES), lambda i: (i, 0)),
            pl.BlockSpec((1, Q_RANK), lambda i: (0, 0)),
            pl.BlockSpec((1, KV_RANK), lambda i: (0, 0)),
            pl.BlockSpec((Q_RANK, hw), lambda i: (0, 0)),
            pl.BlockSpec((N_HEADS * NOPE, KV_RANK), lambda i: (0, 0)),
            pl.BlockSpec((KV_RANK, N_HEADS * V_DIM), lambda i: (0, 0)),
        ],
        out_specs=[
            pl.BlockSpec((tm, hw), lambda i: (i, 0)),
            pl.BlockSpec((None, hw, tm), lambda i: (i, 0, 0)),
            pl.BlockSpec((tm, hw), lambda i: (i, 0)),
        ],
        out_shape=[
            jax.ShapeDtypeStruct((s, hw), BF16),
            jax.ShapeDtypeStruct((s // tm, hw, tm), BF16),
            jax.ShapeDtypeStruct((s, hw), BF16),
        ],
        compiler_params=_cparams(("arbitrary",)),
        name="mla_prep",
    )(proj, cs, qg, kvg, wq, wkt, wv)


def _flash_kernel(q_ref, k_ref, v_ref, o_ref, s_scr, m_scr, acc_scr, *, t, ku):
    i = pl.program_id(1)
    m_scr[...] = jnp.full(m_scr.shape, -jnp.inf, F32)
    acc_scr[...] = jnp.zeros(acc_scr.shape, F32)
    nsub = t // ku
    ncol = ku // LANES

    def scores(j, slot):
        q = q_ref[...]
        for u in range(nsub):
            su = jnp.dot(q, k_ref[j * nsub + u], preferred_element_type=F32)
            for c in range(ncol):
                s_scr[slot, u * ncol + c] = su[:, c * LANES:(c + 1) * LANES]

    def consume(j, slot, masked):
        cols = [s_scr[slot, c] for c in range(nsub * ncol)]
        if masked:
            row = lax.broadcasted_iota(I32, (t, LANES), 0)
            lane = lax.broadcasted_iota(I32, (t, LANES), 1)
            cols = [jnp.where(c * LANES + lane <= row, sc, -jnp.inf) for c, sc in enumerate(cols)]
        m_cur = cols[0]
        for sc in cols[1:]:
            m_cur = jnp.maximum(m_cur, sc)
        m_prev = m_scr[...]
        m_new = jnp.maximum(m_prev, jnp.max(m_cur, axis=-1, keepdims=True))
        alpha = jnp.exp2(m_prev - m_new)
        p = jnp.concatenate([jnp.exp2(sc - m_new).astype(BF16) for sc in cols], axis=1)
        v = v_ref[pl.ds(pl.multiple_of(j * t, t), t), :]
        pv = jnp.dot(p, v, preferred_element_type=F32)
        acc_scr[...] = jnp.concatenate([alpha, alpha], axis=1) * acc_scr[...] + pv
        m_scr[...] = m_new

    scores(0, 0)

    def pair(jj, carry):
        scores(2 * jj + 1, 1)
        consume(2 * jj, 0, False)
        scores(2 * jj + 2, 0)
        consume(2 * jj + 1, 1, False)
        return carry

    lax.fori_loop(0, i // 2, pair, 0)

    @pl.when(i % 2 == 1)
    def _():
        scores(i, 1)
        consume(i - 1, 0, False)
        consume(i, 1, True)

    @pl.when(i % 2 == 0)
    def _():
        consume(i, 0, True)

    acc = acc_scr[...]
    o_ref[...] = acc[:, :V_DIM] / acc[:, V_DIM:V_DIM + 1]


def _flash(q, kt, vp, t=1024):
    s = q.shape[0]
    ku = kt.shape[2]
    t = min(t, s)
    return pl.pallas_call(
        functools.partial(_flash_kernel, t=t, ku=ku),
        grid=(N_HEADS, s // t),
        in_specs=[
            pl.BlockSpec((t, HEAD_W), lambda h, i: (i, h)),
            pl.BlockSpec((s // ku, HEAD_W, ku), lambda h, i: (0, h, 0)),
            pl.BlockSpec((s, HEAD_W), lambda h, i: (0, h)),
        ],
        out_specs=pl.BlockSpec((t, V_DIM), lambda h, i: (i, h)),
        out_shape=jax.ShapeDtypeStruct((s, N_HEADS * V_DIM), F32),
        scratch_shapes=[pltpu.VMEM((2, t // LANES, t, LANES), F32), pltpu.VMEM((t, LANES), F32),
                        pltpu.VMEM((t, HEAD_W), F32)],
        compiler_params=_cparams(("arbitrary", "arbitrary")),
        name="flash_attn",
    )(q, kt, vp)


def _s5_kernel(u_ref, kc_ref, win_ref, wout_ref, al_ref, y_ref, t_scr, ucat_scr, sc_scr, h_scr, *, nct):
    @pl.when(pl.program_id(1) == 0)
    def _():
        t_scr[...] = jnp.zeros(t_scr.shape, BF16)
        for s in range(SSM_L):
            for s2 in range(s, SSM_L):
                t_scr[s * LANES:(s + 1) * LANES, s2 * LANES:(s2 + 1) * LANES] = kc_ref[s2 - s]
        h_scr[...] = jnp.zeros(h_scr.shape, F32)

    for s in range(SSM_L):
        ucat_scr[:, s * LANES:(s + 1) * LANES] = u_ref[pl.ds(s, nct, stride=SSM_L), :].astype(BF16)
    ucat = ucat_scr[...]
    sc_scr[...] = jnp.dot(ucat, win_ref[...], preferred_element_type=F32)
    a_re = al_ref[:, :SLAB_STATE]
    a_im = al_ref[:, SLAB_STATE:]

    def step(c, carry):
        h_re, h_im = carry
        row = sc_scr[pl.ds(c, 1), :]
        sc_scr[pl.ds(c, 1), :] = jnp.concatenate([h_re, h_im], axis=1)
        n_re = a_re * h_re - a_im * h_im + row[:, :SLAB_STATE]
        n_im = a_re * h_im + a_im * h_re + row[:, SLAB_STATE:]
        return n_re, n_im

    h0 = h_scr[...]
    h_re, h_im = lax.fori_loop(0, nct, step, (h0[:, :SLAB_STATE], h0[:, SLAB_STATE:]))
    h_scr[...] = jnp.concatenate([h_re, h_im], axis=1)
    y = (jnp.dot(ucat, t_scr[...], preferred_element_type=F32)
         + jnp.dot(sc_scr[...].astype(BF16), wout_ref[...], preferred_element_type=F32))
    for s in range(SSM_L):
        y_ref[pl.ds(s, nct, stride=SSM_L), :] = y[:, s * LANES:(s + 1) * LANES]


def _s5(proj, u_col_block, kc, win, wout, al, ts=8192):
    s = proj.shape[0]
    ts = min(ts, s)
    nct = ts // SSM_L
    n_slab = kc.shape[0]
    lw = SSM_L * LANES
    return pl.pallas_call(
        functools.partial(_s5_kernel, nct=nct),
        grid=(n_slab, s // ts),
        in_specs=[
            pl.BlockSpec((ts, LANES), lambda j, t: (t, u_col_block + j)),
            pl.BlockSpec((None, SSM_L, LANES, LANES), lambda j, t: (j, 0, 0, 0)),
            pl.BlockSpec((None, lw, 2 * SLAB_STATE), lambda j, t: (j, 0, 0)),
            pl.BlockSpec((None, 2 * SLAB_STATE, lw), lambda j, t: (j, 0, 0)),
            pl.BlockSpec((None, 1, 2 * SLAB_STATE), lambda j, t: (j, 0, 0)),
        ],
        out_specs=pl.BlockSpec((ts, LANES), lambda j, t: (t, j)),
        out_shape=jax.ShapeDtypeStruct((s, n_slab * LANES), F32),
        scratch_shapes=[
            pltpu.VMEM((lw, lw), BF16),
            pltpu.VMEM((nct, lw), BF16),
            pltpu.VMEM((nct, 2 * SLAB_STATE), F32),
            pltpu.VMEM((1, 2 * SLAB_STATE), F32),
        ],
        compiler_params=_cparams(("arbitrary", "arbitrary")),
        name="s5",
    )(proj, kc, win, wout, al)


def _s5_params(lam_re, lam_im, log_dt, b_re, b_im, c_re, c_im, d_skip):
    g, p = lam_re.shape
    h = b_re.shape[-1]
    n_slab = g // SLAB_G
    hp = lax.Precision.HIGHEST
    lr = jnp.minimum(lam_re.astype(F32), -1e-4)
    li = lam_im.astype(F32)
    dt = jnp.exp(log_dt.astype(F32))[:, None]
    kk = jnp.arange(SSM_L + 1, dtype=F32)[:, None, None]
    mag = jnp.exp(lr * dt * kk)
    ang = li * dt * kk
    pw_re, pw_im = mag * jnp.cos(ang), mag * jnp.sin(ang)
    x, y = pw_re[1] - 1.0, pw_im[1]
    den = lr * lr + li * li
    f_re, f_im = (x * lr + y * li) / den, (y * lr - x * li) / den
    bb_re = f_re[..., None] * b_re - f_im[..., None] * b_im
    bb_im = f_re[..., None] * b_im + f_im[..., None] * b_re
    cp_re = c_re[None] * pw_re[:SSM_L, :, None, :] - c_im[None] * pw_im[:SSM_L, :, None, :]
    cp_im = c_re[None] * pw_im[:SSM_L, :, None, :] + c_im[None] * pw_re[:SSM_L, :, None, :]
    kmat = (jnp.einsum('kgop,gpi->gkoi', cp_re, bb_re, precision=hp)
            - jnp.einsum('kgop,gpi->gkoi', cp_im, bb_im, precision=hp))
    kmat = kmat.at[:, 0].add(jnp.eye(h, dtype=F32)[None] * d_skip.astype(F32)[:, :, None])

    def same_group(rows, row_div, cols, col_div):
        r = (jnp.arange(rows, dtype=I32) // row_div) % SLAB_G
        c = (jnp.arange(cols, dtype=I32) // col_div) % SLAB_G
        return r[:, None] == c[None, :]

    km = kmat.reshape(n_slab, SLAB_G, SSM_L, h, h).transpose(0, 2, 1, 4, 3)
    km = jnp.broadcast_to(km.reshape(n_slab, SSM_L, LANES, 1, h), (n_slab, SSM_L, LANES, SLAB_G, h))
    kc = jnp.where(same_group(LANES, h, LANES, h), km.reshape(n_slab, SSM_L, LANES, LANES), 0.0).astype(BF16)
    rk = (SSM_L - 1) - jnp.arange(SSM_L, dtype=F32)[:, None, None]
    rmag, rang = jnp.exp(lr * dt * rk), li * dt * rk
    rp_re, rp_im = rmag * jnp.cos(rang), rmag * jnp.sin(rang)
    wi_re = rp_re[:, :, None, :] * bb_re.transpose(0, 2, 1)[None] - rp_im[:, :, None, :] * bb_im.transpose(0, 2, 1)[None]
    wi_im = rp_re[:, :, None, :] * bb_im.transpose(0, 2, 1)[None] + rp_im[:, :, None, :] * bb_re.transpose(0, 2, 1)[None]

    def slab_in(w):
        w = w.reshape(SSM_L, n_slab, SLAB_G, h, p).transpose(1, 0, 2, 3, 4).reshape(n_slab, SSM_L * LANES, 1, p)
        w = jnp.broadcast_to(w, (n_slab, SSM_L * LANES, SLAB_G, p)).reshape(n_slab, SSM_L * LANES, SLAB_STATE)
        return jnp.where(same_group(SSM_L * LANES, h, SLAB_STATE, p), w, 0.0)

    win = jnp.concatenate([slab_in(wi_re), slab_in(wi_im)], axis=-1).astype(BF16)
    q_re, q_im = pw_re[1:SSM_L + 1], pw_im[1:SSM_L + 1]
    cl_re = c_re[None] * q_re[:, :, None, :] - c_im[None] * q_im[:, :, None, :]
    cl_im = c_re[None] * q_im[:, :, None, :] + c_im[None] * q_re[:, :, None, :]

    def slab_out(w):
        w = w.reshape(SSM_L, n_slab, SLAB_G, h, p).transpose(1, 2, 4, 0, 3).reshape(n_slab, SLAB_STATE, SSM_L, 1, h)
        w = jnp.broadcast_to(w, (n_slab, SLAB_STATE, SSM_L, SLAB_G, h)).reshape(n_slab, SLAB_STATE, SSM_L * LANES)
        return jnp.where(same_group(SLAB_STATE, p, SSM_L * LANES, h), w, 0.0)

    wout = jnp.concatenate([slab_out(cl_re), slab_out(-cl_im)], axis=1).astype(BF16)
    al = jnp.concatenate([pw_re[SSM_L].reshape(n_slab, 1, SLAB_STATE),
                          pw_im[SSM_L].reshape(n_slab, 1, SLAB_STATE)], axis=-1)
    return kc, win, wout, al


def _heads_kernel(attn_ref, y_ref, ag_ref, sg_ref, wglu_ref, bglu_ref, o_ref):
    a = attn_ref[...]
    da = a.shape[1]
    o_ref[:, :da] = (a * lax.rsqrt(jnp.mean(a * a, -1, keepdims=True) + RMS_EPS) * ag_ref[...]).astype(BF16)
    g = jax.nn.gelu(y_ref[...])
    z = jnp.dot(g.astype(BF16), wglu_ref[...], preferred_element_type=F32) + bglu_ref[...]
    ssm = g * _sigmoid(z)
    o_ref[:, da:] = (ssm * lax.rsqrt(jnp.mean(ssm * ssm, -1, keepdims=True) + RMS_EPS) * sg_ref[...]).astype(BF16)


def _heads(attn, y, ag, sg, wglu, bglu, tm=512):
    s, da = attn.shape
    ds = y.shape[1]
    tm = min(tm, s)
    return pl.pallas_call(
        _heads_kernel,
        grid=(s // tm,),
        in_specs=[
            pl.BlockSpec((tm, da), lambda i: (i, 0)),
            pl.BlockSpec((tm, ds), lambda i: (i, 0)),
            pl.BlockSpec((1, da), lambda i: (0, 0)),
            pl.BlockSpec((1, ds), lambda i: (0, 0)),
            pl.BlockSpec((ds, ds), lambda i: (0, 0)),
            pl.BlockSpec((1, ds), lambda i: (0, 0)),
        ],
        out_specs=pl.BlockSpec((tm, da + ds), lambda i: (i, 0)),
        out_shape=jax.ShapeDtypeStruct((s, da + ds), BF16),
        compiler_params=_cparams(("arbitrary",)),
        name="heads",
    )(attn, y, ag, sg, wglu, bglu)


def _layernorm_rows(x, g, b):
    mu = jnp.mean(x, -1, keepdims=True)
    xc = x - mu
    var = jnp.mean(xc * xc, -1, keepdims=True)
    return xc * lax.rsqrt(var + LN_EPS) * g + b


def _outproj_kernel(h_ref, w_ref, x_ref, g_ref, b_ref, o_ref, p_ref, *, alpha, tn, nj):
    j = pl.program_id(1)
    pre = alpha * x_ref[...] + jnp.dot(h_ref[...], w_ref[...], preferred_element_type=F32)
    for jj in range(nj):
        @pl.when(j == jj)
        def _(jj=jj):
            o_ref[:, jj * tn:(jj + 1) * tn] = pre

    @pl.when(j == nj - 1)
    def _():
        xn = _layernorm_rows(o_ref[...], g_ref[...], b_ref[...])
        o_ref[...] = xn
        half = xn.shape[1] // 2
        p_ref[...] = _pack_bf16_pair(xn[:, :half], xn[:, half:])


def _outproj_ln(heads, w, x, g, b, alpha, tm=512, tn=512):
    s, k = heads.shape
    n = w.shape[1]
    tm, tn = min(tm, s), min(tn, n)
    nj = n // tn
    return pl.pallas_call(
        functools.partial(_outproj_kernel, alpha=alpha, tn=tn, nj=nj),
        grid=(s // tm, nj),
        in_specs=[
            pl.BlockSpec((tm, k), lambda i, j: (i, 0)),
            pl.BlockSpec((k, tn), lambda i, j: (0, j)),
            pl.BlockSpec((tm, tn), lambda i, j: (i, j)),
            pl.BlockSpec((1, n), lambda i, j: (0, 0)),
            pl.BlockSpec((1, n), lambda i, j: (0, 0)),
        ],
        out_specs=[pl.BlockSpec((tm, n), lambda i, j: (i, 0)), pl.BlockSpec((tm, n // 2), lambda i, j: (i, 0))],
        out_shape=[jax.ShapeDtypeStruct((s, n), F32), jax.ShapeDtypeStruct((s, n // 2), U32)],
        compiler_params=_cparams(("arbitrary", "arbitrary")),
        name="outproj_ln",
    )(heads, w, x, g, b)


def _first_argmax(v, iota, n):
    m = jnp.max(v, axis=0, keepdims=True)
    first = jnp.min(jnp.where(v == m, iota, n), axis=0, keepdims=True)
    return m, first


def _router_kernel(x_ref, wh_ref, wl_ref, bias_ref, tri_ref, e_ref, g_ref, r_ref, cnt_ref, carry_scr):
    @pl.when(pl.program_id(0) == 0)
    def _():
        carry_scr[...] = jnp.zeros(carry_scr.shape, F32)

    x = x_ref[...]
    xh = x.astype(BF16)
    xl = (x - xh.astype(F32)).astype(BF16)
    dn = (((1,), (1,)), ((), ()))
    wh, wl = wh_ref[...], wl_ref[...]
    logits = (lax.dot_general(wh, xh, dn, preferred_element_type=F32)
              + lax.dot_general(wh, xl, dn, preferred_element_type=F32)
              + lax.dot_general(wl, xh, dn, preferred_element_type=F32))
    scores = _sigmoid(logits)
    sel = scores + bias_ref[...]
    t = sel.shape[1]
    per = N_EXPERTS // N_GROUPS
    ninf = -jnp.inf
    iota_g = lax.broadcasted_iota(I32, (per, t), 0)
    gs_rows = []
    for gi in range(N_GROUPS):
        blk = sel[gi * per:(gi + 1) * per, :]
        m1, f1 = _first_argmax(blk, iota_g, per)
        m2 = jnp.max(jnp.where(iota_g == f1, ninf, blk), axis=0, keepdims=True)
        gs_rows.append(m1 + m2)
    gs = jnp.concatenate(gs_rows, axis=0)
    iota_ng = lax.broadcasted_iota(I32, (N_GROUPS, t), 0)
    gmask = jnp.zeros((N_GROUPS, t), I32)
    for _ in range(TOPK_GROUPS):
        _, f = _first_argmax(gs, iota_ng, N_GROUPS)
        pick = iota_ng == f
        gmask = jnp.where(pick, 1, gmask)
        gs = jnp.where(pick, ninf, gs)
    selm = jnp.concatenate(
        [jnp.where(gmask[gi:gi + 1, :] > 0, sel[gi * per:(gi + 1) * per, :], ninf) for gi in range(N_GROUPS)], axis=0)
    iota_e = lax.broadcasted_iota(I32, (N_EXPERTS, t), 0)
    onehot = jnp.zeros((N_EXPERTS, t), F32)
    e_rows, g_rows = [], []
    for _ in range(TOP_K):
        _, f = _first_argmax(selm, iota_e, N_EXPERTS)
        pick = iota_e == f
        e_rows.append(f)
        g_rows.append(jnp.sum(jnp.where(pick, scores, 0.0), axis=0, keepdims=True))
        onehot = jnp.where(pick, 1.0, onehot)
        selm = jnp.where(pick, ninf, selm)
    gate = jnp.concatenate(g_rows, axis=0)
    gate = gate / jnp.sum(gate, axis=0, keepdims=True) * ROUTED_SCALE
    cum = jnp.dot(onehot.astype(BF16), tri_ref[...], preferred_element_type=F32)
    rank_e = cum - onehot + carry_scr[:, 0:1]
    r_rows = [jnp.sum(jnp.where(iota_e == f, rank_e, 0.0), axis=0, keepdims=True) for f in e_rows]
    carry = carry_scr[...] + cum[:, t - 1:t]
    carry_scr[...] = carry
    e_ref[...] = jnp.concatenate(e_rows, axis=0)
    g_ref[...] = gate
    r_ref[...] = jnp.concatenate(r_rows, axis=0).astype(I32)
    cnt_ref[...] = carry


def _router(x, wh, wl, bias, tm=512):
    t, d = x.shape
    tm = min(tm, t)
    tri = (jnp.arange(tm)[:, None] <= jnp.arange(tm)[None, :]).astype(BF16)
    return pl.pallas_call(
        _router_kernel,
        grid=(t // tm,),
        in_specs=[
            pl.BlockSpec((tm, d), lambda i: (i, 0)),
            pl.BlockSpec((N_EXPERTS, d), lambda i: (0, 0)),
            pl.BlockSpec((N_EXPERTS, d), lambda i: (0, 0)),
            pl.BlockSpec((N_EXPERTS, 1), lambda i: (0, 0)),
            pl.BlockSpec((tm, tm), lambda i: (0, 0)),
        ],
        out_specs=[
            pl.BlockSpec((TOP_K, tm), lambda i: (0, i)),
            pl.BlockSpec((TOP_K, tm), lambda i: (0, i)),
            pl.BlockSpec((TOP_K, tm), lambda i: (0, i)),
            pl.BlockSpec((N_EXPERTS, LANES), lambda i: (0, 0)),
        ],
        out_shape=[
            jax.ShapeDtypeStruct((TOP_K, t), I32),
            jax.ShapeDtypeStruct((TOP_K, t), F32),
            jax.ShapeDtypeStruct((TOP_K, t), I32),
            jax.ShapeDtypeStruct((N_EXPERTS, LANES), F32),
        ],
        scratch_shapes=[pltpu.VMEM((N_EXPERTS, LANES), F32)],
        compiler_params=_cparams(("arbitrary",)),
        name="router",
    )(x, wh, wl, bias, tri)


def _dispatch_kernel(pos_ref, x_ref, xs_hbm, sem, *, tt):
    def issue(t, carry):
        for k in range(TOP_K):
            pltpu.make_async_copy(x_ref.at[pl.ds(t, 1)], xs_hbm.at[pl.ds(pos_ref[k, t], 1)], sem).start(
                priority=k % 2)
        return carry

    lax.fori_loop(0, tt, issue, 0)
    for k in range(TOP_K):
        pltpu.make_async_copy(x_ref, xs_hbm.at[pl.ds(0, tt)], sem).wait()


def _dispatch(pos, xp, tt=512):
    t, w = xp.shape
    tt = min(tt, t)
    return pl.pallas_call(
        functools.partial(_dispatch_kernel, tt=tt),
        grid=(t // tt,),
        in_specs=[
            pl.BlockSpec((TOP_K, tt), lambda i: (0, i), memory_space=pltpu.SMEM),
            pl.BlockSpec((tt, w), lambda i: (i, 0)),
        ],
        out_specs=pl.BlockSpec(memory_space=pl.ANY),
        out_shape=jax.ShapeDtypeStruct((t * TOP_K, w), U32),
        scratch_shapes=[pltpu.SemaphoreType.DMA(())],
        compiler_params=_cparams(("arbitrary",)),
        name="dispatch",
    )(pos, xp)


def _experts_kernel(vblk_ref, vexp_ref, vlo_ref, vhi_ref, vfirst_ref, vnew_ref, xs_ref, wg_ref, wu_ref, wdn_ref,
                    ys_ref, wgu_scr, wd_scr, *, bm):
    v = pl.program_id(0)
    lo, hi = vlo_ref[v], vhi_ref[v]

    @pl.when(vnew_ref[v] == 1)
    def _():
        wgu_scr[:, :D_FF] = wg_ref[...].astype(BF16)
        wgu_scr[:, D_FF:] = wu_ref[...].astype(BF16)
        wd_scr[...] = wdn_ref[...].astype(BF16)

    @pl.when(hi > lo)
    def _():
        xa, xb = _unpack_bf16_pair(xs_ref[...])
        half = xa.shape[1]
        gu = (jnp.dot(xa.astype(BF16), wgu_scr[:half, :], preferred_element_type=F32)
              + jnp.dot(xb.astype(BF16), wgu_scr[half:, :], preferred_element_type=F32))
        hmid = (gu[:, :D_FF] * _sigmoid(gu[:, :D_FF])) * gu[:, D_FF:]
        y = jnp.dot(hmid.astype(BF16), wd_scr[...], preferred_element_type=F32)
        packed = _pack_bf16_pair(y[:, :half], y[:, half:])
        rows = vblk_ref[v] * bm + lax.broadcasted_iota(I32, packed.shape, 0)
        mine = (rows >= lo) & (rows < hi)

        @pl.when(vfirst_ref[v] == 1)
        def _():
            ys_ref[...] = jnp.where(mine, packed, jnp.uint32(0))

        @pl.when(vfirst_ref[v] == 0)
        def _():
            ys_ref[...] = jnp.where(mine, packed, ys_ref[...])


def _experts(sched, xs, w_gate, w_up, w_down, layer, bm):
    r, w = xs.shape
    d = w_down.shape[3]
    nv = sched[0].shape[0]
    grid_spec = pltpu.PrefetchScalarGridSpec(
        num_scalar_prefetch=6,
        grid=(nv,),
        in_specs=[
            pl.BlockSpec((bm, w), lambda v, vb, ve, lo, hi, fi, nw: (vb[v], 0)),
            pl.BlockSpec((None, None, d, D_FF), lambda v, vb, ve, lo, hi, fi, nw: (layer, ve[v], 0, 0)),
            pl.BlockSpec((None, None, d, D_FF), lambda v, vb, ve, lo, hi, fi, nw: (layer, ve[v], 0, 0)),
            pl.BlockSpec((None, None, D_FF, d), lambda v, vb, ve, lo, hi, fi, nw: (layer, ve[v], 0, 0)),
        ],
        out_specs=pl.BlockSpec((bm, w), lambda v, vb, ve, lo, hi, fi, nw: (vb[v], 0)),
        scratch_shapes=[pltpu.VMEM((d, 2 * D_FF), BF16), pltpu.VMEM((D_FF, d), BF16)],
    )
    return pl.pallas_call(
        functools.partial(_experts_kernel, bm=bm),
        grid_spec=grid_spec,
        out_shape=jax.ShapeDtypeStruct((r, w), U32),
        compiler_params=_cparams(("arbitrary",)),
        name="experts",
    )(*sched, xs, w_gate, w_up, w_down)


def _expert_schedule(counts, n_rows, bm):
    counts = counts.astype(I32)
    ends = jnp.cumsum(counts)
    starts = ends - counts
    nb = n_rows // bm
    nv = nb + N_EXPERTS - 1
    first_blk = starts // bm
    last_blk = jnp.maximum(ends - 1, 0) // bm
    nvis = jnp.where(counts > 0, last_blk - first_blk + 1, 0)
    vis_end = jnp.cumsum(nvis)
    vis_start = vis_end - nvis
    total = vis_end[-1]
    v = jnp.arange(nv, dtype=I32)
    e = jnp.minimum(jnp.sum((vis_end[None, :] <= v[:, None]).astype(I32), axis=1), N_EXPERTS - 1)
    blk = first_blk[e] + (v - vis_start[e])
    real = v < total
    last_v = jnp.maximum(total - 1, 0)
    e = jnp.where(real, e, e[last_v])
    blk = jnp.where(real, blk, blk[last_v]).astype(I32)
    lo = jnp.where(real, starts[e], 0).astype(I32)
    hi = jnp.where(real, ends[e], 0).astype(I32)
    prev_blk = jnp.concatenate([jnp.full((1,), -1, I32), blk[:-1]])
    first = (real & (blk != prev_blk)).astype(I32)
    prev_e = jnp.concatenate([jnp.full((1,), -1, I32), e[:-1]])
    new_e = (e != prev_e).astype(I32)
    return starts, (blk, e, lo, hi, first, new_e)


def _combine_kernel(pos_ref, posn_ref, ys_hbm, gate_ref, x_ref, wsgu_ref, wsd_ref, g_ref, b_ref, o_ref, ob_ref,
                    buf, sem, *, tc, alpha, n_steps):
    i = pl.program_id(0)
    slot = i % 2

    def gather(p_ref, dst_slot):
        def issue(t, carry):
            for k in range(TOP_K):
                pltpu.make_async_copy(ys_hbm.at[pl.ds(p_ref[k, t], 1)], buf.at[dst_slot, k, pl.ds(t, 1)],
                                      sem.at[dst_slot]).start(priority=k % 2)
            return carry

        lax.fori_loop(0, tc, issue, 0)

    @pl.when(i == 0)
    def _():
        gather(pos_ref, 0)

    @pl.when(i + 1 < n_steps)
    def _():
        gather(posn_ref, 1 - slot)

    x = x_ref[...]
    xb = x.astype(BF16)
    gu = jnp.dot(xb, wsgu_ref[...], preferred_element_type=F32)
    hmid = (gu[:, :D_FF] * _sigmoid(gu[:, :D_FF])) * gu[:, D_FF:]
    shared = jnp.dot(hmid.astype(BF16), wsd_ref[...], preferred_element_type=F32)
    for k in range(TOP_K):
        pltpu.make_async_copy(ys_hbm.at[pl.ds(0, tc)], buf.at[slot, k], sem.at[slot]).wait()
    half = x.shape[1] // 2
    acc_lo = shared[:, :half]
    acc_hi = shared[:, half:]
    gate = gate_ref[...]
    for k in range(TOP_K):
        lo, hi = _unpack_bf16_pair(buf[slot, k])
        gk = gate[:, k:k + 1]
        acc_lo = acc_lo + gk * lo
        acc_hi = acc_hi + gk * hi
    pre = alpha * x + jnp.concatenate([acc_lo, acc_hi], axis=1)
    xn = _layernorm_rows(pre, g_ref[...], b_ref[...])
    o_ref[...] = xn
    ob_ref[...] = xn.astype(BF16)


def _combine_ln(pos, ys, gate_tok, x, wsgu, wsd, g, b, alpha, tc=128):
    t, d = x.shape
    tc = min(tc, t)
    w = ys.shape[1]
    n_steps = t // tc
    return pl.pallas_call(
        functools.partial(_combine_kernel, tc=tc, alpha=alpha, n_steps=n_steps),
        grid=(n_steps,),
        in_specs=[
            pl.BlockSpec((TOP_K, tc), lambda i: (0, i), memory_space=pltpu.SMEM),
            pl.BlockSpec((TOP_K, tc), lambda i: (0, jnp.minimum(i + 1, n_steps - 1)), memory_space=pltpu.SMEM),
            pl.BlockSpec(memory_space=pl.ANY),
            pl.BlockSpec((tc, TOP_K), lambda i: (i, 0)),
            pl.BlockSpec((tc, d), lambda i: (i, 0)),
            pl.BlockSpec((d, 2 * D_FF), lambda i: (0, 0)),
            pl.BlockSpec((D_FF, d), lambda i: (0, 0)),
            pl.BlockSpec((1, d), lambda i: (0, 0)),
            pl.BlockSpec((1, d), lambda i: (0, 0)),
        ],
        out_specs=[pl.BlockSpec((tc, d), lambda i: (i, 0)), pl.BlockSpec((tc, d), lambda i: (i, 0))],
        out_shape=[jax.ShapeDtypeStruct((t, d), F32), jax.ShapeDtypeStruct((t, d), BF16)],
        scratch_shapes=[pltpu.VMEM((2, TOP_K, tc, w), U32), pltpu.SemaphoreType.DMA((2,))],
        compiler_params=_cparams(("arbitrary",)),
        name="combine_ln",
    )(pos, pos, ys, gate_tok, x, wsgu, wsd, g, b)


def _swap_rope_cols(w):
    half = w.shape[-1] // 2
    return jnp.concatenate([-w[..., half:], w[..., :half]], axis=-1)


def _prep_w_in(w_in):
    off_kr = Q_RANK + KV_RANK
    w_kr = w_in[:, off_kr:off_kr + ROPE]
    return jnp.concatenate(
        [w_in[:, :off_kr], w_kr, _swap_rope_cols(w_kr), w_in[:, off_kr + ROPE:]], axis=1).astype(BF16)


def _prep_w_uq(w_uq):
    w = w_uq.reshape(Q_RANK, N_HEADS, NOPE + ROPE)
    w_r = w[..., NOPE:]
    return jnp.concatenate([w[..., :NOPE], w_r, _swap_rope_cols(w_r)], axis=-1).reshape(
        Q_RANK, N_HEADS * HEAD_W).astype(BF16)


def kernel(x, positions, w_in, q_norm_g, kv_norm_g, w_uq, w_ukv, ssm_lambda_re, ssm_lambda_im, ssm_log_dt,
           ssm_b_re, ssm_b_im, ssm_c_re, ssm_c_im, ssm_d, w_glu, b_glu, attn_out_norm_g, ssm_out_norm_g,
           w_out, ln1_g, ln1_b, w_router, router_bias, w_gate, w_up, w_down, ws_gate, ws_up, ws_down,
           ln2_g, ln2_b):
    batch, seq, d_model = x.shape
    assert batch == 1
    depth = w_in.shape[0]
    alpha = float((2 * depth) ** 0.25)
    u_col_block = (Q_RANK + KV_RANK + LANES) // LANES

    inv_freq = ROPE_THETA ** (-(jnp.arange(0, ROPE, 2, dtype=F32) / ROPE))
    ang = positions.astype(F32)[0][:, None] * inv_freq
    cos, sin = jnp.cos(ang), jnp.sin(ang)
    cs = jnp.concatenate([cos, cos, sin, sin], axis=1)

    xf = x[0]
    xb = xf
    for l in range(depth):
        proj = _matmul(xb, _prep_w_in(w_in[l]), F32, tm=512, tn=1152)
        w_kv = w_ukv[l].reshape(KV_RANK, N_HEADS, NOPE + V_DIM)
        wkt = w_kv[..., :NOPE].reshape(KV_RANK, N_HEADS * NOPE).T.astype(BF16)
        wv = w_kv[..., NOPE:].reshape(KV_RANK, N_HEADS * V_DIM).astype(BF16)
        q, kt, vp = _mla_prep(proj, cs, q_norm_g[l][None], kv_norm_g[l][None],
                              _prep_w_uq(w_uq[l]), wkt, wv, min(MLA_TM, seq))
        attn = _flash(q, kt, vp)
        kc, win, wout, al = _s5_params(ssm_lambda_re[l], ssm_lambda_im[l], ssm_log_dt[l], ssm_b_re[l],
                                       ssm_b_im[l], ssm_c_re[l], ssm_c_im[l], ssm_d[l])
        y = _s5(proj, u_col_block, kc, win, wout, al)
        heads = _heads(attn, y, attn_out_norm_g[l][None], ssm_out_norm_g[l][None],
                       w_glu[l].astype(BF16), b_glu[l][None])
        x1, x1p = _outproj_ln(heads, w_out[l].astype(BF16), xf, ln1_g[l][None], ln1_b[l][None], alpha)
        wr_t = w_router[l].T
        wr_h = wr_t.astype(BF16)
        wr_l = (wr_t - wr_h.astype(F32)).astype(BF16)
        top_e, gate, rank, cnt = _router(x1, wr_h, wr_l, router_bias[l][:, None])
        starts, sched = _expert_schedule(cnt[:, 0], seq * TOP_K, min(MOE_BM, seq * TOP_K))
        e_ids = jnp.arange(N_EXPERTS, dtype=I32)[:, None, None]
        pos = rank + jnp.sum(jnp.where(top_e[None] == e_ids, starts[:, None, None], 0), axis=0)
        xs = _dispatch(pos, x1p)
        ys = _experts(sched, xs, w_gate, w_up, w_down, l, min(MOE_BM, seq * TOP_K))
        wsgu = jnp.concatenate([ws_gate[l], ws_up[l]], axis=-1).astype(BF16)
        xf, xb = _combine_ln(pos, ys, gate.T, x1, wsgu, ws_down[l].astype(BF16),
                             ln2_g[l][None], ln2_b[l][None], alpha)
    return xf[None]
```

```python
import functools
import math

import jax
import jax.numpy as jnp
from jax import lax
from jax.experimental import pallas as pl
from jax.experimental.pallas import tpu as pltpu

F32 = jnp.float32
BF16 = jnp.bfloat16
I32 = jnp.int32
U32 = jnp.uint32

N_HEADS = 16
NOPE = 128
ROPE = 64
V_DIM = 128
HEAD_W = 256
Q_RANK = 768
KV_RANK = 512
ROPE_THETA = 10000.0
SSM_H = 16
SSM_P = 64
N_EXPERTS = 64
D_FF = 256
TOP_K = 8
N_GROUPS = 8
TOPK_GROUPS = 4
ROUTED_SCALE = 2.5
LN_EPS = 1e-5
RMS_EPS = 1e-6

LANES = 128
V7X_VMEM_LIMIT = 56 * 1024 * 1024
SSM_L = 16
SLAB_G = LANES // SSM_H
SLAB_STATE = SLAB_G * SSM_P
MOE_BM = 256
MOE_CHUNK = 512
MLA_TM = 256


def _cparams(sem):
    return pltpu.CompilerParams(dimension_semantics=sem, vmem_limit_bytes=V7X_VMEM_LIMIT)


def _sigmoid(z):
    return 1.0 / (1.0 + jnp.exp(-z))


def _pack_bf16_pair(lo, hi):
    lo_bits = lax.bitcast_convert_type(lo.astype(BF16).astype(F32), U32)
    hi_bits = lax.bitcast_convert_type(hi.astype(BF16).astype(F32), U32)
    return (lo_bits >> 16) | (hi_bits & jnp.uint32(0xFFFF0000))


def _unpack_bf16_pair(w):
    lo = lax.bitcast_convert_type(w << 16, F32)
    hi = lax.bitcast_convert_type(w & jnp.uint32(0xFFFF0000), F32)
    return lo, hi


def _mm_kernel(a_ref, b_ref, o_ref):
    o_ref[...] = jnp.dot(a_ref[...].astype(BF16), b_ref[...], preferred_element_type=F32).astype(o_ref.dtype)


def _matmul(a, b, out_dtype, tm, tn):
    m, k = a.shape
    n = b.shape[1]
    tm, tn = min(tm, m), min(tn, n)
    return pl.pallas_call(
        _mm_kernel,
        grid=(n // tn, m // tm),
        in_specs=[pl.BlockSpec((tm, k), lambda j, i: (i, 0)), pl.BlockSpec((k, tn), lambda j, i: (0, j))],
        out_specs=pl.BlockSpec((tm, tn), lambda j, i: (i, j)),
        out_shape=jax.ShapeDtypeStruct((m, n), out_dtype),
        compiler_params=_cparams(("arbitrary", "arbitrary")),
        name="proj_in",
    )(a, b)


def _rope_pair(t):
    r = t + pltpu.roll(t, 64, axis=1)
    lane = lax.broadcasted_iota(I32, r.shape, 1)
    return jnp.where(lane < ROPE, r, 0.0)


def _mla_prep_kernel(c_ref, cs_ref, qg_ref, kvg_ref, wq_ref, wkt_ref, wv_ref, q_ref, kt_ref, vp_ref, *, scale):
    cq = c_ref[:, 0:Q_RANK]
    ckv = c_ref[:, Q_RANK:Q_RANK + KV_RANK]
    kblk = c_ref[:, Q_RANK + KV_RANK:Q_RANK + KV_RANK + LANES]
    cs = cs_ref[...]
    cqn = (cq * lax.rsqrt(jnp.mean(cq * cq, -1, keepdims=True) + RMS_EPS) * qg_ref[...]).astype(BF16)
    ckvn = (ckv * lax.rsqrt(jnp.mean(ckv * ckv, -1, keepdims=True) + RMS_EPS) * kvg_ref[...]).astype(BF16)
    k_rope_t = jnp.transpose(_rope_pair(kblk * cs)).astype(BF16)
    lane = lax.broadcasted_iota(I32, (cq.shape[0], LANES), 1)
    ones_col = jnp.where(lane == 0, 1.0, 0.0).astype(BF16)
    nt = (((1,), (1,)), ((), ()))
    for h in range(N_HEADS):
        qh = jnp.dot(cqn, wq_ref[:, h * HEAD_W:(h + 1) * HEAD_W], preferred_element_type=F32)
        q_ref[:, h * HEAD_W:h * HEAD_W + NOPE] = (qh[:, :NOPE] * scale).astype(BF16)
        q_ref[:, h * HEAD_W + NOPE:(h + 1) * HEAD_W] = (_rope_pair(qh[:, NOPE:] * cs) * scale).astype(BF16)
        kt = lax.dot_general(wkt_ref[h * NOPE:(h + 1) * NOPE, :], ckvn, nt, preferred_element_type=F32)
        kt_ref[h * HEAD_W:h * HEAD_W + NOPE, :] = kt.astype(BF16)
        kt_ref[h * HEAD_W + NOPE:(h + 1) * HEAD_W, :] = k_rope_t
        vh = jnp.dot(ckvn, wv_ref[:, h * V_DIM:(h + 1) * V_DIM], preferred_element_type=F32)
        vp_ref[:, h * HEAD_W:h * HEAD_W + V_DIM] = vh.astype(BF16)
        vp_ref[:, h * HEAD_W + V_DIM:(h + 1) * HEAD_W] = ones_col


def _mla_prep(proj, cs, qg, kvg, wq, wkt, wv, tm):
    s = proj.shape[0]
    cw = Q_RANK + KV_RANK + LANES
    scale = float((NOPE + ROPE) ** -0.5 * math.log2(math.e))
    hw = N_HEADS * HEAD_W
    return pl.pallas_call(
        functools.partial(_mla_prep_kernel, scale=scale),
        grid=(s // tm,),
        in_specs=[
            pl.BlockSpec((tm, cw), lambda i: (i, 0)),
            pl.BlockSpec((tm, LANES), lambda i: (i, 0)),
            pl.BlockSpec((1, Q_RANK), lambda i: (0, 0)),
            pl.BlockSpec((1, KV_RANK), lambda i: (0, 0)),
            pl.BlockSpec((Q_RANK, hw), lambda i: (0, 0)),
            pl.BlockSpec((N_HEADS * NOPE, KV_RANK), lambda i: (0, 0)),
            pl.BlockSpec((KV_RANK, N_HEADS * V_DIM), lambda i: (0, 0)),
        ],
        out_specs=[
            pl.BlockSpec((tm, hw), lambda i: (i, 0)),
            pl.BlockSpec((None, hw, tm), lambda i: (i, 0, 0)),
            pl.BlockSpec((tm, hw), lambda i: (i, 0)),
        ],
        out_shape=[
            jax.ShapeDtypeStruct((s, hw), BF16),
            jax.ShapeDtypeStruct((s // tm, hw, tm), BF16),
            jax.ShapeDtypeStruct((s, hw), BF16),
        ],
        compiler_params=_cparams(("arbitrary",)),
        name="mla_prep",
    )(proj, cs, qg, kvg, wq, wkt, wv)


def _flash_kernel(q_ref, k_ref, v_ref, o_ref, s_scr, m_scr, acc_scr, *, t, ku):
    i = pl.program_id(1)
    m_scr[...] = jnp.full(m_scr.shape, -jnp.inf, F32)
    acc_scr[...] = jnp.zeros(acc_scr.shape, F32)
    nsub = t // ku
    ncol = ku // LANES

    def scores(j, slot):
        q = q_ref[...]
        for u in range(nsub):
            su = jnp.dot(q, k_ref[j * nsub + u], preferred_element_type=F32)
            for c in range(ncol):
                s_scr[slot, u * ncol + c] = su[:, c * LANES:(c + 1) * LANES]

    def consume(j, slot, masked):
        cols = [s_scr[slot, c] for c in range(nsub * ncol)]
        if masked:
            row = lax.broadcasted_iota(I32, (t, LANES), 0)
            lane = lax.broadcasted_iota(I32, (t, LANES), 1)
            cols = [jnp.where(c * LANES + lane <= row, sc, -jnp.inf) for c, sc in enumerate(cols)]
        m_cur = cols[0]
        for sc in cols[1:]:
            m_cur = jnp.maximum(m_cur, sc)
        m_prev = m_scr[...]
        m_new = jnp.maximum(m_prev, jnp.max(m_cur, axis=-1, keepdims=True))
        alpha = jnp.exp2(m_prev - m_new)
        p = jnp.concatenate([jnp.exp2(sc - m_new).astype(BF16) for sc in cols], axis=1)
        v = v_ref[pl.ds(pl.multiple_of(j * t, t), t), :]
        pv = jnp.dot(p, v, preferred_element_type=F32)
        acc_scr[...] = jnp.concatenate([alpha, alpha], axis=1) * acc_scr[...] + pv
        m_scr[...] = m_new

    scores(0, 0)

    def pair(jj, carry):
        scores(2 * jj + 1, 1)
        consume(2 * jj, 0, False)
        scores(2 * jj + 2, 0)
        consume(2 * jj + 1, 1, False)
        return carry

    lax.fori_loop(0, i // 2, pair, 0)

    @pl.when(i % 2 == 1)
    def _():
        scores(i, 1)
        consume(i - 1, 0, False)
        consume(i, 1, True)

    @pl.when(i % 2 == 0)
    def _():
        consume(i, 0, True)

    acc = acc_scr[...]
    o_ref[...] = acc[:, :V_DIM] / acc[:, V_DIM:V_DIM + 1]


def _flash(q, kt, vp, t=1024):
    s = q.shape[0]
    ku = kt.shape[2]
    t = min(t, s)
    return pl.pallas_call(
        functools.partial(_flash_kernel, t=t, ku=ku),
        grid=(N_HEADS, s // t),
        in_specs=[
            pl.BlockSpec((t, HEAD_W), lambda h, i: (i, h)),
            pl.BlockSpec((s // ku, HEAD_W, ku), lambda h, i: (0, h, 0)),
            pl.BlockSpec((s, HEAD_W), lambda h, i: (0, h)),
        ],
        out_specs=pl.BlockSpec((t, V_DIM), lambda h, i: (i, h)),
        out_shape=jax.ShapeDtypeStruct((s, N_HEADS * V_DIM), F32),
        scratch_shapes=[pltpu.VMEM((2, t // LANES, t, LANES), F32), pltpu.VMEM((t, LANES), F32),
                        pltpu.VMEM((t, HEAD_W), F32)],
        compiler_params=_cparams(("arbitrary", "arbitrary")),
        name="flash_attn",
    )(q, kt, vp)


def _s5_kernel(u_ref, kc_ref, win_ref, wout_ref, al_ref, y_ref, t_scr, ucat_scr, sc_scr, h_scr, *, nct):
    @pl.when(pl.program_id(1) == 0)
    def _():
        t_scr[...] = jnp.zeros(t_scr.shape, BF16)
        for s in range(SSM_L):
            for s2 in range(s, SSM_L):
                t_scr[s * LANES:(s + 1) * LANES, s2 * LANES:(s2 + 1) * LANES] = kc_ref[s2 - s]
        h_scr[...] = jnp.zeros(h_scr.shape, F32)

    for s in range(SSM_L):
        ucat_scr[:, s * LANES:(s + 1) * LANES] = u_ref[pl.ds(s, nct, stride=SSM_L), :].astype(BF16)
    ucat = ucat_scr[...]
    sc_scr[...] = jnp.dot(ucat, win_ref[...], preferred_element_type=F32)
    a_re = al_ref[:, :SLAB_STATE]
    a_im = al_ref[:, SLAB_STATE:]

    def step(c, carry):
        h_re, h_im = carry
        row = sc_scr[pl.ds(c, 1), :]
        sc_scr[pl.ds(c, 1), :] = jnp.concatenate([h_re, h_im], axis=1)
        n_re = a_re * h_re - a_im * h_im + row[:, :SLAB_STATE]
        n_im = a_re * h_im + a_im * h_re + row[:, SLAB_STATE:]
        return n_re, n_im

    h0 = h_scr[...]
    h_re, h_im = lax.fori_loop(0, nct, step, (h0[:, :SLAB_STATE], h0[:, SLAB_STATE:]))
    h_scr[...] = jnp.concatenate([h_re, h_im], axis=1)
    hprev = sc_scr[...].astype(BF16)
    for s in range(0, SSM_L, 2):
        kext = (s + 2) * LANES
        cols = slice(s * LANES, (s + 2) * LANES)
        y = (jnp.dot(ucat_scr[:, :kext], t_scr[:kext, cols], preferred_element_type=F32)
             + jnp.dot(hprev, wout_ref[:, cols], preferred_element_type=F32))
        y_ref[pl.ds(s, nct, stride=SSM_L), :] = y[:, :LANES]
        y_ref[pl.ds(s + 1, nct, stride=SSM_L), :] = y[:, LANES:]


def _s5(proj, u_col_block, kc, win, wout, al, ts=8192):
    s = proj.shape[0]
    ts = min(ts, s)
    nct = ts // SSM_L
    n_slab = kc.shape[0]
    lw = SSM_L * LANES
    return pl.pallas_call(
        functools.partial(_s5_kernel, nct=nct),
        grid=(n_slab, s // ts),
        in_specs=[
            pl.BlockSpec((ts, LANES), lambda j, t: (t, u_col_block + j)),
            pl.BlockSpec((None, SSM_L, LANES, LANES), lambda j, t: (j, 0, 0, 0)),
            pl.BlockSpec((None, lw, 2 * SLAB_STATE), lambda j, t: (j, 0, 0)),
            pl.BlockSpec((None, 2 * SLAB_STATE, lw), lambda j, t: (j, 0, 0)),
            pl.BlockSpec((None, 1, 2 * SLAB_STATE), lambda j, t: (j, 0, 0)),
        ],
        out_specs=pl.BlockSpec((ts, LANES), lambda j, t: (t, j)),
        out_shape=jax.ShapeDtypeStruct((s, n_slab * LANES), F32),
        scratch_shapes=[
            pltpu.VMEM((lw, lw), BF16),
            pltpu.VMEM((nct, lw), BF16),
            pltpu.VMEM((nct, 2 * SLAB_STATE), F32),
            pltpu.VMEM((1, 2 * SLAB_STATE), F32),
        ],
        compiler_params=_cparams(("arbitrary", "arbitrary")),
        name="s5",
    )(proj, kc, win, wout, al)


def _s5_params(lam_re, lam_im, log_dt, b_re, b_im, c_re, c_im, d_skip):
    g, p = lam_re.shape
    h = b_re.shape[-1]
    n_slab = g // SLAB_G
    hp = lax.Precision.HIGHEST
    lr = jnp.minimum(lam_re.astype(F32), -1e-4)
    li = lam_im.astype(F32)
    dt = jnp.exp(log_dt.astype(F32))[:, None]
    kk = jnp.arange(SSM_L + 1, dtype=F32)[:, None, None]
    mag = jnp.exp(lr * dt * kk)
    ang = li * dt * kk
    pw_re, pw_im = mag * jnp.cos(ang), mag * jnp.sin(ang)
    x, y = pw_re[1] - 1.0, pw_im[1]
    den = lr * lr + li * li
    f_re, f_im = (x * lr + y * li) / den, (y * lr - x * li) / den
    bb_re = f_re[..., None] * b_re - f_im[..., None] * b_im
    bb_im = f_re[..., None] * b_im + f_im[..., None] * b_re
    cp_re = c_re[None] * pw_re[:SSM_L, :, None, :] - c_im[None] * pw_im[:SSM_L, :, None, :]
    cp_im = c_re[None] * pw_im[:SSM_L, :, None, :] + c_im[None] * pw_re[:SSM_L, :, None, :]
    kmat = (jnp.einsum('kgop,gpi->gkoi', cp_re, bb_re, precision=hp)
            - jnp.einsum('kgop,gpi->gkoi', cp_im, bb_im, precision=hp))
    kmat = kmat.at[:, 0].add(jnp.eye(h, dtype=F32)[None] * d_skip.astype(F32)[:, :, None])

    def same_group(rows, row_div, cols, col_div):
        r = (jnp.arange(rows, dtype=I32) // row_div) % SLAB_G
        c = (jnp.arange(cols, dtype=I32) // col_div) % SLAB_G
        return r[:, None] == c[None, :]

    km = kmat.reshape(n_slab, SLAB_G, SSM_L, h, h).transpose(0, 2, 1, 4, 3)
    km = jnp.broadcast_to(km.reshape(n_slab, SSM_L, LANES, 1, h), (n_slab, SSM_L, LANES, SLAB_G, h))
    kc = jnp.where(same_group(LANES, h, LANES, h), km.reshape(n_slab, SSM_L, LANES, LANES), 0.0).astype(BF16)
    rk = (SSM_L - 1) - jnp.arange(SSM_L, dtype=F32)[:, None, None]
    rmag, rang = jnp.exp(lr * dt * rk), li * dt * rk
    rp_re, rp_im = rmag * jnp.cos(rang), rmag * jnp.sin(rang)
    wi_re = rp_re[:, :, None, :] * bb_re.transpose(0, 2, 1)[None] - rp_im[:, :, None, :] * bb_im.transpose(0, 2, 1)[None]
    wi_im = rp_re[:, :, None, :] * bb_im.transpose(0, 2, 1)[None] + rp_im[:, :, None, :] * bb_re.transpose(0, 2, 1)[None]

    def slab_in(w):
        w = w.reshape(SSM_L, n_slab, SLAB_G, h, p).transpose(1, 0, 2, 3, 4).reshape(n_slab, SSM_L * LANES, 1, p)
        w = jnp.broadcast_to(w, (n_slab, SSM_L * LANES, SLAB_G, p)).reshape(n_slab, SSM_L * LANES, SLAB_STATE)
        return jnp.where(same_group(SSM_L * LANES, h, SLAB_STATE, p), w, 0.0)

    win = jnp.concatenate([slab_in(wi_re), slab_in(wi_im)], axis=-1).astype(BF16)
    q_re, q_im = pw_re[1:SSM_L + 1], pw_im[1:SSM_L + 1]
    cl_re = c_re[None] * q_re[:, :, None, :] - c_im[None] * q_im[:, :, None, :]
    cl_im = c_re[None] * q_im[:, :, None, :] + c_im[None] * q_re[:, :, None, :]

    def slab_out(w):
        w = w.reshape(SSM_L, n_slab, SLAB_G, h, p).transpose(1, 2, 4, 0, 3).reshape(n_slab, SLAB_STATE, SSM_L, 1, h)
        w = jnp.broadcast_to(w, (n_slab, SLAB_STATE, SSM_L, SLAB_G, h)).reshape(n_slab, SLAB_STATE, SSM_L * LANES)
        return jnp.where(same_group(SLAB_STATE, p, SSM_L * LANES, h), w, 0.0)

    wout = jnp.concatenate([slab_out(cl_re), slab_out(-cl_im)], axis=1).astype(BF16)
    al = jnp.concatenate([pw_re[SSM_L].reshape(n_slab, 1, SLAB_STATE),
                          pw_im[SSM_L].reshape(n_slab, 1, SLAB_STATE)], axis=-1)
    return kc, win, wout, al


def _heads_kernel(attn_ref, y_ref, ag_ref, sg_ref, wglu_ref, bglu_ref, o_ref):
    a = attn_ref[...]
    da = a.shape[1]
    o_ref[:, :da] = (a * lax.rsqrt(jnp.mean(a * a, -1, keepdims=True) + RMS_EPS) * ag_ref[...]).astype(BF16)
    g = jax.nn.gelu(y_ref[...])
    z = jnp.dot(g.astype(BF16), wglu_ref[...], preferred_element_type=F32) + bglu_ref[...]
    ssm = g * _sigmoid(z)
    o_ref[:, da:] = (ssm * lax.rsqrt(jnp.mean(ssm * ssm, -1, keepdims=True) + RMS_EPS) * sg_ref[...]).astype(BF16)


def _heads(attn, y, ag, sg, wglu, bglu, tm=512):
    s, da = attn.shape
    ds = y.shape[1]
    tm = min(tm, s)
    return pl.pallas_call(
        _heads_kernel,
        grid=(s // tm,),
        in_specs=[
            pl.BlockSpec((tm, da), lambda i: (i, 0)),
            pl.BlockSpec((tm, ds), lambda i: (i, 0)),
            pl.BlockSpec((1, da), lambda i: (0, 0)),
            pl.BlockSpec((1, ds), lambda i: (0, 0)),
            pl.BlockSpec((ds, ds), lambda i: (0, 0)),
            pl.BlockSpec((1, ds), lambda i: (0, 0)),
        ],
        out_specs=pl.BlockSpec((tm, da + ds), lambda i: (i, 0)),
        out_shape=jax.ShapeDtypeStruct((s, da + ds), BF16),
        compiler_params=_cparams(("arbitrary",)),
        name="heads",
    )(attn, y, ag, sg, wglu, bglu)


def _layernorm_rows(x, g, b):
    mu = jnp.mean(x, -1, keepdims=True)
    xc = x - mu
    var = jnp.mean(xc * xc, -1, keepdims=True)
    return xc * lax.rsqrt(var + LN_EPS) * g + b


def _outproj_kernel(h_ref, w_ref, x_ref, g_ref, b_ref, o_ref, p_ref, *, alpha, tn, nj):
    j = pl.program_id(1)
    pre = alpha * x_ref[...] + jnp.dot(h_ref[...], w_ref[...], preferred_element_type=F32)
    for jj in range(nj):
        @pl.when(j == jj)
        def _(jj=jj):
            o_ref[:, jj * tn:(jj + 1) * tn] = pre

    @pl.when(j == nj - 1)
    def _():
        xn = _layernorm_rows(o_ref[...], g_ref[...], b_ref[...])
        o_ref[...] = xn
        half = xn.shape[1] // 2
        p_ref[...] = _pack_bf16_pair(xn[:, :half], xn[:, half:])


def _outproj_ln(heads, w, x, g, b, alpha, tm=512, tn=512):
    s, k = heads.shape
    n = w.shape[1]
    tm, tn = min(tm, s), min(tn, n)
    nj = n // tn
    return pl.pallas_call(
        functools.partial(_outproj_kernel, alpha=alpha, tn=tn, nj=nj),
        grid=(s // tm, nj),
        in_specs=[
            pl.BlockSpec((tm, k), lambda i, j: (i, 0)),
            pl.BlockSpec((k, tn), lambda i, j: (0, j)),
            pl.BlockSpec((tm, tn), lambda i, j: (i, j)),
            pl.BlockSpec((1, n), lambda i, j: (0, 0)),
            pl.BlockSpec((1, n), lambda i, j: (0, 0)),
        ],
        out_specs=[pl.BlockSpec((tm, n), lambda i, j: (i, 0)), pl.BlockSpec((tm, n // 2), lambda i, j: (i, 0))],
        out_shape=[jax.ShapeDtypeStruct((s, n), F32), jax.ShapeDtypeStruct((s, n // 2), U32)],
        compiler_params=_cparams(("arbitrary", "arbitrary")),
        name="outproj_ln",
    )(heads, w, x, g, b)


def _first_argmax(v, iota, n):
    m = jnp.max(v, axis=0, keepdims=True)
    first = jnp.min(jnp.where(v == m, iota, n), axis=0, keepdims=True)
    return m, first


def _router_kernel(x_ref, wh_ref, wl_ref, bias_ref, tri_ref, e_ref, g_ref, r_ref, cnt_ref, carry_scr):
    @pl.when(pl.program_id(0) == 0)
    def _():
        carry_scr[...] = jnp.zeros(carry_scr.shape, F32)

    x = x_ref[...]
    xh = x.astype(BF16)
    xl = (x - xh.astype(F32)).astype(BF16)
    dn = (((1,), (1,)), ((), ()))
    wh, wl = wh_ref[...], wl_ref[...]
    logits = (lax.dot_general(wh, xh, dn, preferred_element_type=F32)
              + lax.dot_general(wh, xl, dn, preferred_element_type=F32)
              + lax.dot_general(wl, xh, dn, preferred_element_type=F32))
    scores = _sigmoid(logits)
    sel = scores + bias_ref[...]
    t = sel.shape[1]
    per = N_EXPERTS // N_GROUPS
    ninf = -jnp.inf
    iota_g = lax.broadcasted_iota(I32, (per, t), 0)
    gs_rows = []
    for gi in range(N_GROUPS):
        blk = sel[gi * per:(gi + 1) * per, :]
        m1, f1 = _first_argmax(blk, iota_g, per)
        m2 = jnp.max(jnp.where(iota_g == f1, ninf, blk), axis=0, keepdims=True)
        gs_rows.append(m1 + m2)
    gs = jnp.concatenate(gs_rows, axis=0)
    iota_ng = lax.broadcasted_iota(I32, (N_GROUPS, t), 0)
    gmask = jnp.zeros((N_GROUPS, t), I32)
    for _ in range(TOPK_GROUPS):
        _, f = _first_argmax(gs, iota_ng, N_GROUPS)
        pick = iota_ng == f
        gmask = jnp.where(pick, 1, gmask)
        gs = jnp.where(pick, ninf, gs)
    selm = jnp.concatenate(
        [jnp.where(gmask[gi:gi + 1, :] > 0, sel[gi * per:(gi + 1) * per, :], ninf) for gi in range(N_GROUPS)], axis=0)
    iota_e = lax.broadcasted_iota(I32, (N_EXPERTS, t), 0)
    onehot = jnp.zeros((N_EXPERTS, t), F32)
    e_rows, g_rows = [], []
    for _ in range(TOP_K):
        _, f = _first_argmax(selm, iota_e, N_EXPERTS)
        pick = iota_e == f
        e_rows.append(f)
        g_rows.append(jnp.sum(jnp.where(pick, scores, 0.0), axis=0, keepdims=True))
        onehot = jnp.where(pick, 1.0, onehot)
        selm = jnp.where(pick, ninf, selm)
    gate = jnp.concatenate(g_rows, axis=0)
    gate = gate / jnp.sum(gate, axis=0, keepdims=True) * ROUTED_SCALE
    cum = jnp.dot(onehot.astype(BF16), tri_ref[...], preferred_element_type=F32)
    rank_e = cum - onehot + carry_scr[:, 0:1]
    r_rows = [jnp.sum(jnp.where(iota_e == f, rank_e, 0.0), axis=0, keepdims=True) for f in e_rows]
    carry = carry_scr[...] + cum[:, t - 1:t]
    carry_scr[...] = carry
    e_ref[...] = jnp.concatenate(e_rows, axis=0)
    g_ref[...] = gate
    r_ref[...] = jnp.concatenate(r_rows, axis=0).astype(I32)
    cnt_ref[...] = carry


def _router(x, wh, wl, bias, tm=512):
    t, d = x.shape
    tm = min(tm, t)
    tri = (jnp.arange(tm)[:, None] <= jnp.arange(tm)[None, :]).astype(BF16)
    return pl.pallas_call(
        _router_kernel,
        grid=(t // tm,),
        in_specs=[
            pl.BlockSpec((tm, d), lambda i: (i, 0)),
            pl.BlockSpec((N_EXPERTS, d), lambda i: (0, 0)),
            pl.BlockSpec((N_EXPERTS, d), lambda i: (0, 0)),
            pl.BlockSpec((N_EXPERTS, 1), lambda i: (0, 0)),
            pl.BlockSpec((tm, tm), lambda i: (0, 0)),
        ],
        out_specs=[
            pl.BlockSpec((TOP_K, tm), lambda i: (0, i)),
            pl.BlockSpec((TOP_K, tm), lambda i: (0, i)),
            pl.BlockSpec((TOP_K, tm), lambda i: (0, i)),
            pl.BlockSpec((N_EXPERTS, LANES), lambda i: (0, 0)),
        ],
        out_shape=[
            jax.ShapeDtypeStruct((TOP_K, t), I32),
            jax.ShapeDtypeStruct((TOP_K, t), F32),
            jax.ShapeDtypeStruct((TOP_K, t), I32),
            jax.ShapeDtypeStruct((N_EXPERTS, LANES), F32),
        ],
        scratch_shapes=[pltpu.VMEM((N_EXPERTS, LANES), F32)],
        compiler_params=_cparams(("arbitrary",)),
        name="router",
    )(x, wh, wl, bias, tri)


def _dispatch_kernel(pos_ref, x_ref, xs_hbm, sem, *, tt):
    def issue(t, carry):
        for k in range(TOP_K):
            pltpu.make_async_copy(x_ref.at[pl.ds(t, 1)], xs_hbm.at[pl.ds(pos_ref[k, t], 1)], sem).start(
                priority=k % 2)
        return carry

    lax.fori_loop(0, tt, issue, 0)
    for k in range(TOP_K):
        pltpu.make_async_copy(x_ref, xs_hbm.at[pl.ds(0, tt)], sem).wait()


def _dispatch(pos, xp, tt=512):
    t, w = xp.shape
    tt = min(tt, t)
    return pl.pallas_call(
        functools.partial(_dispatch_kernel, tt=tt),
        grid=(t // tt,),
        in_specs=[
            pl.BlockSpec((TOP_K, tt), lambda i: (0, i), memory_space=pltpu.SMEM),
            pl.BlockSpec((tt, w), lambda i: (i, 0)),
        ],
        out_specs=pl.BlockSpec(memory_space=pl.ANY),
        out_shape=jax.ShapeDtypeStruct((t * TOP_K, w), U32),
        scratch_shapes=[pltpu.SemaphoreType.DMA(())],
        compiler_params=_cparams(("arbitrary",)),
        name="dispatch",
    )(pos, xp)


def _experts_kernel(vblk_ref, vexp_ref, vlo_ref, vhi_ref, vfirst_ref, vnew_ref, xs_ref, wg_ref, wu_ref, wdn_ref,
                    ys_ref, wgu_scr, wd_scr, *, bm):
    v = pl.program_id(0)
    lo, hi = vlo_ref[v], vhi_ref[v]

    @pl.when(vnew_ref[v] == 1)
    def _():
        wgu_scr[:, :D_FF] = wg_ref[...].astype(BF16)
        wgu_scr[:, D_FF:] = wu_ref[...].astype(BF16)
        wd_scr[...] = wdn_ref[...].astype(BF16)

    @pl.when(vfirst_ref[v] == 1)
    def _():
        ys_ref[...] = jnp.zeros(ys_ref.shape, U32)

    @pl.when(hi > lo)
    def _():
        half = xs_ref.shape[1]
        gu = None
        for c in range(half // MOE_CHUNK):
            cs_ = slice(c * MOE_CHUNK, (c + 1) * MOE_CHUNK)
            cs_hi = slice(half + c * MOE_CHUNK, half + (c + 1) * MOE_CHUNK)
            xa, xb = _unpack_bf16_pair(xs_ref[:, cs_])
            part = (jnp.dot(xa.astype(BF16), wgu_scr[cs_, :], preferred_element_type=F32)
                    + jnp.dot(xb.astype(BF16), wgu_scr[cs_hi, :], preferred_element_type=F32))
            gu = part if gu is None else gu + part
        hmid = ((gu[:, :D_FF] * _sigmoid(gu[:, :D_FF])) * gu[:, D_FF:]).astype(BF16)
        rows = vblk_ref[v] * bm + lax.broadcasted_iota(I32, (bm, MOE_CHUNK), 0)
        mine = (rows >= lo) & (rows < hi)
        for c in range(half // MOE_CHUNK):
            cs_ = slice(c * MOE_CHUNK, (c + 1) * MOE_CHUNK)
            cs_hi = slice(half + c * MOE_CHUNK, half + (c + 1) * MOE_CHUNK)
            y_lo = jnp.dot(hmid, wd_scr[:, cs_], preferred_element_type=F32)
            y_hi = jnp.dot(hmid, wd_scr[:, cs_hi], preferred_element_type=F32)
            ys_ref[:, cs_] = jnp.where(mine, _pack_bf16_pair(y_lo, y_hi), ys_ref[:, cs_])


def _experts(sched, xs, w_gate, w_up, w_down, layer, bm):
    r, w = xs.shape
    d = w_down.shape[3]
    nv = sched[0].shape[0]
    grid_spec = pltpu.PrefetchScalarGridSpec(
        num_scalar_prefetch=6,
        grid=(nv,),
        in_specs=[
            pl.BlockSpec((bm, w), lambda v, vb, ve, lo, hi, fi, nw: (vb[v], 0)),
            pl.BlockSpec((None, None, d, D_FF), lambda v, vb, ve, lo, hi, fi, nw: (layer, ve[v], 0, 0)),
            pl.BlockSpec((None, None, d, D_FF), lambda v, vb, ve, lo, hi, fi, nw: (layer, ve[v], 0, 0)),
            pl.BlockSpec((None, None, D_FF, d), lambda v, vb, ve, lo, hi, fi, nw: (layer, ve[v], 0, 0)),
        ],
        out_specs=pl.BlockSpec((bm, w), lambda v, vb, ve, lo, hi, fi, nw: (vb[v], 0)),
        scratch_shapes=[pltpu.VMEM((d, 2 * D_FF), BF16), pltpu.VMEM((D_FF, d), BF16)],
    )
    return pl.pallas_call(
        functools.partial(_experts_kernel, bm=bm),
        grid_spec=grid_spec,
        out_shape=jax.ShapeDtypeStruct((r, w), U32),
        compiler_params=_cparams(("arbitrary",)),
        name="experts",
    )(*sched, xs, w_gate, w_up, w_down)


def _expert_schedule(counts, n_rows, bm):
    counts = counts.astype(I32)
    ends = jnp.cumsum(counts)
    starts = ends - counts
    nb = n_rows // bm
    nv = nb + N_EXPERTS - 1
    first_blk = starts // bm
    last_blk = jnp.maximum(ends - 1, 0) // bm
    nvis = jnp.where(counts > 0, last_blk - first_blk + 1, 0)
    vis_end = jnp.cumsum(nvis)
    vis_start = vis_end - nvis
    total = vis_end[-1]
    v = jnp.arange(nv, dtype=I32)
    e = jnp.minimum(jnp.sum((vis_end[None, :] <= v[:, None]).astype(I32), axis=1), N_EXPERTS - 1)
    real = v < total
    e = jnp.where(real, e, jnp.max(jnp.where(real, e, 0)))
    is_e = e[:, None] == jnp.arange(N_EXPERTS, dtype=I32)[None, :]

    def per_visit(table):
        return jnp.sum(jnp.where(is_e, table[None, :], 0), axis=1)

    blk = per_visit(first_blk) + (v - per_visit(vis_start))
    blk = jnp.where(real, blk, jnp.max(jnp.where(real, blk, 0))).astype(I32)
    lo = jnp.where(real, per_visit(starts), 0).astype(I32)
    hi = jnp.where(real, per_visit(ends), 0).astype(I32)
    prev_blk = jnp.concatenate([jnp.full((1,), -1, I32), blk[:-1]])
    first = (real & (blk != prev_blk)).astype(I32)
    prev_e = jnp.concatenate([jnp.full((1,), -1, I32), e[:-1]])
    new_e = (e != prev_e).astype(I32)
    return starts, (blk, e, lo, hi, first, new_e)


def _combine_kernel(pos_ref, posn_ref, ys_hbm, gate_ref, x_ref, wsgu_ref, wsd_ref, g_ref, b_ref, o_ref, ob_ref,
                    buf, sem, *, tc, alpha, n_steps):
    i = pl.program_id(0)
    slot = i % 2

    def gather(p_ref, dst_slot):
        def issue(t, carry):
            for k in range(TOP_K):
                pltpu.make_async_copy(ys_hbm.at[pl.ds(p_ref[k, t], 1)], buf.at[dst_slot, k, pl.ds(t, 1)],
                                      sem.at[dst_slot]).start(priority=k % 2)
            return carry

        lax.fori_loop(0, tc, issue, 0)

    @pl.when(i == 0)
    def _():
        gather(pos_ref, 0)

    @pl.when(i + 1 < n_steps)
    def _():
        gather(posn_ref, 1 - slot)

    x = x_ref[...]
    xb = x.astype(BF16)
    gu = jnp.dot(xb, wsgu_ref[...], preferred_element_type=F32)
    hmid = (gu[:, :D_FF] * _sigmoid(gu[:, :D_FF])) * gu[:, D_FF:]
    shared = jnp.dot(hmid.astype(BF16), wsd_ref[...], preferred_element_type=F32)
    for k in range(TOP_K):
        pltpu.make_async_copy(ys_hbm.at[pl.ds(0, tc)], buf.at[slot, k], sem.at[slot]).wait()
    d = x.shape[1]
    half = d // 2
    gate = gate_ref[...]
    gks = [jnp.broadcast_to(gate[:, k:k + 1], (tc, LANES)) for k in range(TOP_K)]
    rsum = jnp.zeros((tc, LANES), F32)
    for c in range(half // LANES):
        sl = slice(c * LANES, (c + 1) * LANES)
        sh = slice(half + c * LANES, half + (c + 1) * LANES)
        a_lo, a_hi = shared[:, sl], shared[:, sh]
        for k in range(TOP_K):
            lo, hi = _unpack_bf16_pair(buf[slot, k, :, sl])
            a_lo = a_lo + gks[k] * lo
            a_hi = a_hi + gks[k] * hi
        p_lo = alpha * x_ref[:, sl] + a_lo
        p_hi = alpha * x_ref[:, sh] + a_hi
        o_ref[:, sl] = p_lo
        o_ref[:, sh] = p_hi
        rsum = rsum + (p_lo + p_hi)
    mu = jnp.broadcast_to(jnp.sum(rsum, -1, keepdims=True) * (1.0 / d), (tc, LANES))
    rsq = jnp.zeros((tc, LANES), F32)
    for c in range(d // LANES):
        xc = o_ref[:, c * LANES:(c + 1) * LANES] - mu
        rsq = rsq + xc * xc
    inv = jnp.broadcast_to(lax.rsqrt(jnp.sum(rsq, -1, keepdims=True) * (1.0 / d) + LN_EPS), (tc, LANES))
    for c in range(d // LANES):
        sl = slice(c * LANES, (c + 1) * LANES)
        xn = (o_ref[:, sl] - mu) * inv * g_ref[:, sl] + b_ref[:, sl]
        o_ref[:, sl] = xn
        ob_ref[:, sl] = xn.astype(BF16)


def _combine_ln(pos, ys, gate_tok, x, wsgu, wsd, g, b, alpha, tc=128):
    t, d = x.shape
    tc = min(tc, t)
    w = ys.shape[1]
    n_steps = t // tc
    return pl.pallas_call(
        functools.partial(_combine_kernel, tc=tc, alpha=alpha, n_steps=n_steps),
        grid=(n_steps,),
        in_specs=[
            pl.BlockSpec((TOP_K, tc), lambda i: (0, i), memory_space=pltpu.SMEM),
            pl.BlockSpec((TOP_K, tc), lambda i: (0, jnp.minimum(i + 1, n_steps - 1)), memory_space=pltpu.SMEM),
            pl.BlockSpec(memory_space=pl.ANY),
            pl.BlockSpec((tc, TOP_K), lambda i: (i, 0)),
            pl.BlockSpec((tc, d), lambda i: (i, 0)),
            pl.BlockSpec((d, 2 * D_FF), lambda i: (0, 0)),
            pl.BlockSpec((D_FF, d), lambda i: (0, 0)),
            pl.BlockSpec((1, d), lambda i: (0, 0)),
            pl.BlockSpec((1, d), lambda i: (0, 0)),
        ],
        out_specs=[pl.BlockSpec((tc, d), lambda i: (i, 0)), pl.BlockSpec((tc, d), lambda i: (i, 0))],
        out_shape=[jax.ShapeDtypeStruct((t, d), F32), jax.ShapeDtypeStruct((t, d), BF16)],
        scratch_shapes=[pltpu.VMEM((2, TOP_K, tc, w), U32), pltpu.SemaphoreType.DMA((2,))],
        compiler_params=_cparams(("arbitrary",)),
        name="combine_ln",
    )(pos, pos, ys, gate_tok, x, wsgu, wsd, g, b)


def _swap_rope_cols(w):
    half = w.shape[-1] // 2
    return jnp.concatenate([-w[..., half:], w[..., :half]], axis=-1)


def _prep_w_in(w_in):
    off_kr = Q_RANK + KV_RANK
    w_kr = w_in[:, off_kr:off_kr + ROPE]
    return jnp.concatenate(
        [w_in[:, :off_kr], w_kr, _swap_rope_cols(w_kr), w_in[:, off_kr + ROPE:]], axis=1).astype(BF16)


def _prep_w_uq(w_uq):
    w = w_uq.reshape(Q_RANK, N_HEADS, NOPE + ROPE)
    w_r = w[..., NOPE:]
    return jnp.concatenate([w[..., :NOPE], w_r, _swap_rope_cols(w_r)], axis=-1).reshape(
        Q_RANK, N_HEADS * HEAD_W).astype(BF16)


def kernel(x, positions, w_in, q_norm_g, kv_norm_g, w_uq, w_ukv, ssm_lambda_re, ssm_lambda_im, ssm_log_dt,
           ssm_b_re, ssm_b_im, ssm_c_re, ssm_c_im, ssm_d, w_glu, b_glu, attn_out_norm_g, ssm_out_norm_g,
           w_out, ln1_g, ln1_b, w_router, router_bias, w_gate, w_up, w_down, ws_gate, ws_up, ws_down,
           ln2_g, ln2_b):
    batch, seq, d_model = x.shape
    assert batch == 1
    depth = w_in.shape[0]
    alpha = float((2 * depth) ** 0.25)
    u_col_block = (Q_RANK + KV_RANK + LANES) // LANES

    inv_freq = ROPE_THETA ** (-(jnp.arange(0, ROPE, 2, dtype=F32) / ROPE))
    ang = positions.astype(F32)[0][:, None] * inv_freq
    cos, sin = jnp.cos(ang), jnp.sin(ang)
    cs = jnp.concatenate([cos, cos, sin, sin], axis=1)

    xf = x[0]
    xb = xf
    for l in range(depth):
        proj = _matmul(xb, _prep_w_in(w_in[l]), F32, tm=512, tn=1152)
        w_kv = w_ukv[l].reshape(KV_RANK, N_HEADS, NOPE + V_DIM)
        wkt = w_kv[..., :NOPE].reshape(KV_RANK, N_HEADS * NOPE).T.astype(BF16)
        wv = w_kv[..., NOPE:].reshape(KV_RANK, N_HEADS * V_DIM).astype(BF16)
        q, kt, vp = _mla_prep(proj, cs, q_norm_g[l][None], kv_norm_g[l][None],
                              _prep_w_uq(w_uq[l]), wkt, wv, min(MLA_TM, seq))
        attn = _flash(q, kt, vp)
        kc, win, wout, al = _s5_params(ssm_lambda_re[l], ssm_lambda_im[l], ssm_log_dt[l], ssm_b_re[l],
                                       ssm_b_im[l], ssm_c_re[l], ssm_c_im[l], ssm_d[l])
        y = _s5(proj, u_col_block, kc, win, wout, al)
        heads = _heads(attn, y, attn_out_norm_g[l][None], ssm_out_norm_g[l][None],
                       w_glu[l].astype(BF16), b_glu[l][None])
        x1, x1p = _outproj_ln(heads, w_out[l].astype(BF16), xf, ln1_g[l][None], ln1_b[l][None], alpha)
        wr_t = w_router[l].T
        wr_h = wr_t.astype(BF16)
        wr_l = (wr_t - wr_h.astype(F32)).astype(BF16)
        top_e, gate, rank, cnt = _router(x1, wr_h, wr_l, router_bias[l][:, None])
        starts, sched = _expert_schedule(cnt[:, 0], seq * TOP_K, min(MOE_BM, seq * TOP_K))
        e_ids = jnp.arange(N_EXPERTS, dtype=I32)[:, None, None]
        pos = rank + jnp.sum(jnp.where(top_e[None] == e_ids, starts[:, None, None], 0), axis=0)
        xs = _dispatch(pos, x1p)
        ys = _experts(sched, xs, w_gate, w_up, w_down, l, min(MOE_BM, seq * TOP_K))
        wsgu = jnp.concatenate([ws_gate[l], ws_up[l]], axis=-1).astype(BF16)
        xf, xb = _combine_ln(pos, ys, gate.T, x1, wsgu, ws_down[l].astype(BF16),
                             ln2_g[l][None], ln2_b[l][None], alpha)
    return xf[None]
```

```python
import functools
import math

import jax
import jax.numpy as jnp
from jax import lax
from jax.experimental import pallas as pl
from jax.experimental.pallas import tpu as pltpu

F32 = jnp.float32
BF16 = jnp.bfloat16
I32 = jnp.int32
U32 = jnp.uint32

N_HEADS = 16
NOPE = 128
ROPE = 64
V_DIM = 128
HEAD_W = 256
Q_RANK = 768
KV_RANK = 512
ROPE_THETA = 10000.0
SSM_H = 16
SSM_P = 64
N_EXPERTS = 64
D_FF = 256
TOP_K = 8
N_GROUPS = 8
TOPK_GROUPS = 4
ROUTED_SCALE = 2.5
LN_EPS = 1e-5
RMS_EPS = 1e-6

LANES = 128
V7X_VMEM_LIMIT = 56 * 1024 * 1024
SSM_L = 16
SLAB_G = LANES // SSM_H
SLAB_STATE = SLAB_G * SSM_P
MOE_BM = 256
MOE_CHUNK = 512
MLA_TM = 256


def _cparams(sem):
    return pltpu.CompilerParams(dimension_semantics=sem, vmem_limit_bytes=V7X_VMEM_LIMIT)


def _sigmoid(z):
    return 1.0 / (1.0 + jnp.exp(-z))


def _pack_bf16_pair(lo, hi):
    lo_bits = lax.bitcast_convert_type(lo.astype(BF16).astype(F32), U32)
    hi_bits = lax.bitcast_convert_type(hi.astype(BF16).astype(F32), U32)
    return (lo_bits >> 16) | (hi_bits & jnp.uint32(0xFFFF0000))


def _unpack_bf16_pair(w):
    lo = lax.bitcast_convert_type(w << 16, F32)
    hi = lax.bitcast_convert_type(w & jnp.uint32(0xFFFF0000), F32)
    return lo, hi


def _mm_kernel(a_ref, b_ref, o_ref):
    o_ref[...] = jnp.dot(a_ref[...].astype(BF16), b_ref[...], preferred_element_type=F32).astype(o_ref.dtype)


def _matmul(a, b, out_dtype, tm, tn):
    m, k = a.shape
    n = b.shape[1]
    tm, tn = min(tm, m), min(tn, n)
    return pl.pallas_call(
        _mm_kernel,
        grid=(n // tn, m // tm),
        in_specs=[pl.BlockSpec((tm, k), lambda j, i: (i, 0)), pl.BlockSpec((k, tn), lambda j, i: (0, j))],
        out_specs=pl.BlockSpec((tm, tn), lambda j, i: (i, j)),
        out_shape=jax.ShapeDtypeStruct((m, n), out_dtype),
        compiler_params=_cparams(("arbitrary", "arbitrary")),
        name="proj_in",
    )(a, b)


def _rope_pair(t):
    r = t + pltpu.roll(t, 64, axis=1)
    lane = lax.broadcasted_iota(I32, r.shape, 1)
    return jnp.where(lane < ROPE, r, 0.0)


def _mla_prep_kernel(c_ref, cs_ref, qg_ref, kvg_ref, wq_ref, wkt_ref, wv_ref, q_ref, kt_ref, vp_ref, *, scale):
    cq = c_ref[:, 0:Q_RANK]
    ckv = c_ref[:, Q_RANK:Q_RANK + KV_RANK]
    kblk = c_ref[:, Q_RANK + KV_RANK:Q_RANK + KV_RANK + LANES]
    cs = cs_ref[...]
    cqn = (cq * lax.rsqrt(jnp.mean(cq * cq, -1, keepdims=True) + RMS_EPS) * qg_ref[...]).astype(BF16)
    ckvn = (ckv * lax.rsqrt(jnp.mean(ckv * ckv, -1, keepdims=True) + RMS_EPS) * kvg_ref[...]).astype(BF16)
    k_rope_t = jnp.transpose(_rope_pair(kblk * cs)).astype(BF16)
    lane = lax.broadcasted_iota(I32, (cq.shape[0], LANES), 1)
    ones_col = jnp.where(lane == 0, 1.0, 0.0).astype(BF16)
    nt = (((1,), (1,)), ((), ()))
    for h in range(N_HEADS):
        qh = jnp.dot(cqn, wq_ref[:, h * HEAD_W:(h + 1) * HEAD_W], preferred_element_type=F32)
        q_ref[:, h * HEAD_W:h * HEAD_W + NOPE] = (qh[:, :NOPE] * scale).astype(BF16)
        q_ref[:, h * HEAD_W + NOPE:(h + 1) * HEAD_W] = (_rope_pair(qh[:, NOPE:] * cs) * scale).astype(BF16)
        kt = lax.dot_general(wkt_ref[h * NOPE:(h + 1) * NOPE, :], ckvn, nt, preferred_element_type=F32)
        kt_ref[h * HEAD_W:h * HEAD_W + NOPE, :] = kt.astype(BF16)
        kt_ref[h * HEAD_W + NOPE:(h + 1) * HEAD_W, :] = k_rope_t
        vh = jnp.dot(ckvn, wv_ref[:, h * V_DIM:(h + 1) * V_DIM], preferred_element_type=F32)
        vp_ref[:, h * HEAD_W:h * HEAD_W + V_DIM] = vh.astype(BF16)
        vp_ref[:, h * HEAD_W + V_DIM:(h + 1) * HEAD_W] = ones_col


def _mla_prep(proj, cs, qg, kvg, wq, wkt, wv, tm):
    s = proj.shape[0]
    cw = Q_RANK + KV_RANK + LANES
    scale = float((NOPE + ROPE) ** -0.5 * math.log2(math.e))
    hw = N_HEADS * HEAD_W
    return pl.pallas_call(
        functools.partial(_mla_prep_kernel, scale=scale),
        grid=(s // tm,),
        in_specs=[
            pl.BlockSpec((tm, cw), lambda i: (i, 0)),
            pl.BlockSpec((tm, LANES), lambda i: (i, 0)),
            pl.BlockSpec((1, Q_RANK), lambda i: (0, 0)),
            pl.BlockSpec((1, KV_RANK), lambda i: (0, 0)),
            pl.BlockSpec((Q_RANK, hw), lambda i: (0, 0)),
            pl.BlockSpec((N_HEADS * NOPE, KV_RANK), lambda i: (0, 0)),
            pl.BlockSpec((KV_RANK, N_HEADS * V_DIM), lambda i: (0, 0)),
        ],
        out_specs=[
            pl.BlockSpec((tm, hw), lambda i: (i, 0)),
            pl.BlockSpec((None, hw, tm), lambda i: (i, 0, 0)),
            pl.BlockSpec((tm, hw), lambda i: (i, 0)),
        ],
        out_shape=[
            jax.ShapeDtypeStruct((s, hw), BF16),
            jax.ShapeDtypeStruct((s // tm, hw, tm), BF16),
            jax.ShapeDtypeStruct((s, hw), BF16),
        ],
        compiler_params=_cparams(("arbitrary",)),
        name="mla_prep",
    )(proj, cs, qg, kvg, wq, wkt, wv)


def _flash_kernel(q_ref, k_ref, v_ref, o_ref, s_scr, m_scr, acc_scr, *, t, ku):
    i = pl.program_id(1)
    m_scr[...] = jnp.full(m_scr.shape, -jnp.inf, F32)
    acc_scr[...] = jnp.zeros(acc_scr.shape, F32)
    nsub = t // ku
    ncol = ku // LANES

    def scores(j, slot):
        q = q_ref[...]
        for u in range(nsub):
            su = jnp.dot(q, k_ref[j * nsub + u], preferred_element_type=F32)
            for c in range(ncol):
                s_scr[slot, u * ncol + c] = su[:, c * LANES:(c + 1) * LANES]

    def consume(j, slot, masked):
        cols = [s_scr[slot, c] for c in range(nsub * ncol)]
        if masked:
            row = lax.broadcasted_iota(I32, (t, LANES), 0)
            lane = lax.broadcasted_iota(I32, (t, LANES), 1)
            cols = [jnp.where(c * LANES + lane <= row, sc, -jnp.inf) for c, sc in enumerate(cols)]
        m_cur = cols[0]
        for sc in cols[1:]:
            m_cur = jnp.maximum(m_cur, sc)
        m_prev = m_scr[...]
        m_new = jnp.maximum(m_prev, jnp.max(m_cur, axis=-1, keepdims=True))
        alpha = jnp.exp2(m_prev - m_new)
        p = jnp.concatenate([jnp.exp2(sc - m_new).astype(BF16) for sc in cols], axis=1)
        v = v_ref[pl.ds(pl.multiple_of(j * t, t), t), :]
        pv = jnp.dot(p, v, preferred_element_type=F32)
        acc_scr[...] = jnp.concatenate([alpha, alpha], axis=1) * acc_scr[...] + pv
        m_scr[...] = m_new

    scores(0, 0)

    def pair(jj, carry):
        scores(2 * jj + 1, 1)
        consume(2 * jj, 0, False)
        scores(2 * jj + 2, 0)
        consume(2 * jj + 1, 1, False)
        return carry

    lax.fori_loop(0, i // 2, pair, 0)

    @pl.when(i % 2 == 1)
    def _():
        scores(i, 1)
        consume(i - 1, 0, False)
        consume(i, 1, True)

    @pl.when(i % 2 == 0)
    def _():
        consume(i, 0, True)

    acc = acc_scr[...]
    o_ref[...] = acc[:, :V_DIM] / acc[:, V_DIM:V_DIM + 1]


def _flash(q, kt, vp, t=1024):
    s = q.shape[0]
    ku = kt.shape[2]
    t = min(t, s)
    return pl.pallas_call(
        functools.partial(_flash_kernel, t=t, ku=ku),
        grid=(N_HEADS, s // t),
        in_specs=[
            pl.BlockSpec((t, HEAD_W), lambda h, i: (i, h)),
            pl.BlockSpec((s // ku, HEAD_W, ku), lambda h, i: (0, h, 0)),
            pl.BlockSpec((s, HEAD_W), lambda h, i: (0, h)),
        ],
        out_specs=pl.BlockSpec((t, V_DIM), lambda h, i: (i, h)),
        out_shape=jax.ShapeDtypeStruct((s, N_HEADS * V_DIM), F32),
        scratch_shapes=[pltpu.VMEM((2, t // LANES, t, LANES), F32), pltpu.VMEM((t, LANES), F32),
                        pltpu.VMEM((t, HEAD_W), F32)],
        compiler_params=_cparams(("arbitrary", "arbitrary")),
        name="flash_attn",
    )(q, kt, vp)


def _s5_kernel(u_ref, kc_ref, win_ref, wout_ref, al_ref, y_ref, t_scr, ucat_scr, sc_scr, h_scr, *, nct):
    @pl.when(pl.program_id(1) == 0)
    def _():
        t_scr[...] = jnp.zeros(t_scr.shape, BF16)
        for s in range(SSM_L):
            for s2 in range(s, SSM_L):
                t_scr[s * LANES:(s + 1) * LANES, s2 * LANES:(s2 + 1) * LANES] = kc_ref[s2 - s]
        h_scr[...] = jnp.zeros(h_scr.shape, F32)

    for s in range(SSM_L):
        ucat_scr[:, s * LANES:(s + 1) * LANES] = u_ref[pl.ds(s, nct, stride=SSM_L), :].astype(BF16)
    ucat = ucat_scr[...]
    sc_scr[...] = jnp.dot(ucat, win_ref[...], preferred_element_type=F32)
    a_re = al_ref[:, :SLAB_STATE]
    a_im = al_ref[:, SLAB_STATE:]

    def step(c, carry):
        h_re, h_im = carry
        row = sc_scr[pl.ds(c, 1), :]
        sc_scr[pl.ds(c, 1), :] = jnp.concatenate([h_re, h_im], axis=1)
        n_re = a_re * h_re - a_im * h_im + row[:, :SLAB_STATE]
        n_im = a_re * h_im + a_im * h_re + row[:, SLAB_STATE:]
        return n_re, n_im

    h0 = h_scr[...]
    h_re, h_im = lax.fori_loop(0, nct, step, (h0[:, :SLAB_STATE], h0[:, SLAB_STATE:]))
    h_scr[...] = jnp.concatenate([h_re, h_im], axis=1)
    hprev = sc_scr[...].astype(BF16)
    for s in range(0, SSM_L, 2):
        kext = (s + 2) * LANES
        cols = slice(s * LANES, (s + 2) * LANES)
        y = (jnp.dot(ucat_scr[:, :kext], t_scr[:kext, cols], preferred_element_type=F32)
             + jnp.dot(hprev, wout_ref[:, cols], preferred_element_type=F32))
        y_ref[pl.ds(s, nct, stride=SSM_L), :] = y[:, :LANES]
        y_ref[pl.ds(s + 1, nct, stride=SSM_L), :] = y[:, LANES:]


def _s5(proj, u_col_block, tables, layer, ts=8192):
    kc, win, wout, al = tables
    s = proj.shape[0]
    ts = min(ts, s)
    nct = ts // SSM_L
    n_slab = kc.shape[1]
    lw = SSM_L * LANES
    return pl.pallas_call(
        functools.partial(_s5_kernel, nct=nct),
        grid=(n_slab, s // ts),
        in_specs=[
            pl.BlockSpec((ts, LANES), lambda j, t: (t, u_col_block + j)),
            pl.BlockSpec((None, None, SSM_L, LANES, LANES), lambda j, t: (layer, j, 0, 0, 0)),
            pl.BlockSpec((None, None, lw, 2 * SLAB_STATE), lambda j, t: (layer, j, 0, 0)),
            pl.BlockSpec((None, None, 2 * SLAB_STATE, lw), lambda j, t: (layer, j, 0, 0)),
            pl.BlockSpec((None, None, 1, 2 * SLAB_STATE), lambda j, t: (layer, j, 0, 0)),
        ],
        out_specs=pl.BlockSpec((ts, LANES), lambda j, t: (t, j)),
        out_shape=jax.ShapeDtypeStruct((s, n_slab * LANES), F32),
        scratch_shapes=[
            pltpu.VMEM((lw, lw), BF16),
            pltpu.VMEM((nct, lw), BF16),
            pltpu.VMEM((nct, 2 * SLAB_STATE), F32),
            pltpu.VMEM((1, 2 * SLAB_STATE), F32),
        ],
        compiler_params=_cparams(("arbitrary", "arbitrary")),
        name="s5",
    )(proj, kc, win, wout, al)


def _s5_params(lam_re, lam_im, log_dt, b_re, b_im, c_re, c_im, d_skip):
    g, p = lam_re.shape
    h = b_re.shape[-1]
    n_slab = g // SLAB_G
    hp = lax.Precision.HIGHEST
    lr = jnp.minimum(lam_re.astype(F32), -1e-4)
    li = lam_im.astype(F32)
    dt = jnp.exp(log_dt.astype(F32))[:, None]
    kk = jnp.arange(SSM_L + 1, dtype=F32)[:, None, None]
    mag = jnp.exp(lr * dt * kk)
    ang = li * dt * kk
    pw_re, pw_im = mag * jnp.cos(ang), mag * jnp.sin(ang)
    x, y = pw_re[1] - 1.0, pw_im[1]
    den = lr * lr + li * li
    f_re, f_im = (x * lr + y * li) / den, (y * lr - x * li) / den
    bb_re = f_re[..., None] * b_re - f_im[..., None] * b_im
    bb_im = f_re[..., None] * b_im + f_im[..., None] * b_re
    cp_re = c_re[None] * pw_re[:SSM_L, :, None, :] - c_im[None] * pw_im[:SSM_L, :, None, :]
    cp_im = c_re[None] * pw_im[:SSM_L, :, None, :] + c_im[None] * pw_re[:SSM_L, :, None, :]
    kmat = (jnp.einsum('kgop,gpi->gkoi', cp_re, bb_re, precision=hp)
            - jnp.einsum('kgop,gpi->gkoi', cp_im, bb_im, precision=hp))
    kmat = kmat.at[:, 0].add(jnp.eye(h, dtype=F32)[None] * d_skip.astype(F32)[:, :, None])

    def same_group(rows, row_div, cols, col_div):
        r = (jnp.arange(rows, dtype=I32) // row_div) % SLAB_G
        c = (jnp.arange(cols, dtype=I32) // col_div) % SLAB_G
        return r[:, None] == c[None, :]

    km = kmat.reshape(n_slab, SLAB_G, SSM_L, h, h).transpose(0, 2, 1, 4, 3)
    km = jnp.broadcast_to(km.reshape(n_slab, SSM_L, LANES, 1, h), (n_slab, SSM_L, LANES, SLAB_G, h))
    kc = jnp.where(same_group(LANES, h, LANES, h), km.reshape(n_slab, SSM_L, LANES, LANES), 0.0).astype(BF16)
    rk = (SSM_L - 1) - jnp.arange(SSM_L, dtype=F32)[:, None, None]
    rmag, rang = jnp.exp(lr * dt * rk), li * dt * rk
    rp_re, rp_im = rmag * jnp.cos(rang), rmag * jnp.sin(rang)
    wi_re = rp_re[:, :, None, :] * bb_re.transpose(0, 2, 1)[None] - rp_im[:, :, None, :] * bb_im.transpose(0, 2, 1)[None]
    wi_im = rp_re[:, :, None, :] * bb_im.transpose(0, 2, 1)[None] + rp_im[:, :, None, :] * bb_re.transpose(0, 2, 1)[None]

    def slab_in(w):
        w = w.reshape(SSM_L, n_slab, SLAB_G, h, p).transpose(1, 0, 2, 3, 4).reshape(n_slab, SSM_L * LANES, 1, p)
        w = jnp.broadcast_to(w, (n_slab, SSM_L * LANES, SLAB_G, p)).reshape(n_slab, SSM_L * LANES, SLAB_STATE)
        return jnp.where(same_group(SSM_L * LANES, h, SLAB_STATE, p), w, 0.0)

    win = jnp.concatenate([slab_in(wi_re), slab_in(wi_im)], axis=-1).astype(BF16)
    q_re, q_im = pw_re[1:SSM_L + 1], pw_im[1:SSM_L + 1]
    cl_re = c_re[None] * q_re[:, :, None, :] - c_im[None] * q_im[:, :, None, :]
    cl_im = c_re[None] * q_im[:, :, None, :] + c_im[None] * q_re[:, :, None, :]

    def slab_out(w):
        w = w.reshape(SSM_L, n_slab, SLAB_G, h, p).transpose(1, 2, 4, 0, 3).reshape(n_slab, SLAB_STATE, SSM_L, 1, h)
        w = jnp.broadcast_to(w, (n_slab, SLAB_STATE, SSM_L, SLAB_G, h)).reshape(n_slab, SLAB_STATE, SSM_L * LANES)
        return jnp.where(same_group(SLAB_STATE, p, SSM_L * LANES, h), w, 0.0)

    wout = jnp.concatenate([slab_out(cl_re), slab_out(-cl_im)], axis=1).astype(BF16)
    al = jnp.concatenate([pw_re[SSM_L].reshape(n_slab, 1, SLAB_STATE),
                          pw_im[SSM_L].reshape(n_slab, 1, SLAB_STATE)], axis=-1)
    return kc, win, wout, al


def _heads_kernel(attn_ref, y_ref, ag_ref, sg_ref, wglu_ref, bglu_ref, o_ref):
    a = attn_ref[...]
    da = a.shape[1]
    o_ref[:, :da] = (a * lax.rsqrt(jnp.mean(a * a, -1, keepdims=True) + RMS_EPS) * ag_ref[...]).astype(BF16)
    g = jax.nn.gelu(y_ref[...])
    z = jnp.dot(g.astype(BF16), wglu_ref[...], preferred_element_type=F32) + bglu_ref[...]
    ssm = g * _sigmoid(z)
    o_ref[:, da:] = (ssm * lax.rsqrt(jnp.mean(ssm * ssm, -1, keepdims=True) + RMS_EPS) * sg_ref[...]).astype(BF16)


def _heads(attn, y, ag, sg, wglu, bglu, tm=512):
    s, da = attn.shape
    ds = y.shape[1]
    tm = min(tm, s)
    return pl.pallas_call(
        _heads_kernel,
        grid=(s // tm,),
        in_specs=[
            pl.BlockSpec((tm, da), lambda i: (i, 0)),
            pl.BlockSpec((tm, ds), lambda i: (i, 0)),
            pl.BlockSpec((1, da), lambda i: (0, 0)),
            pl.BlockSpec((1, ds), lambda i: (0, 0)),
            pl.BlockSpec((ds, ds), lambda i: (0, 0)),
            pl.BlockSpec((1, ds), lambda i: (0, 0)),
        ],
        out_specs=pl.BlockSpec((tm, da + ds), lambda i: (i, 0)),
        out_shape=jax.ShapeDtypeStruct((s, da + ds), BF16),
        compiler_params=_cparams(("arbitrary",)),
        name="heads",
    )(attn, y, ag, sg, wglu, bglu)


def _layernorm_rows(x, g, b):
    mu = jnp.mean(x, -1, keepdims=True)
    xc = x - mu
    var = jnp.mean(xc * xc, -1, keepdims=True)
    return xc * lax.rsqrt(var + LN_EPS) * g + b


def _outproj_kernel(h_ref, w_ref, x_ref, g_ref, b_ref, o_ref, p_ref, *, alpha, tn, nj):
    j = pl.program_id(1)
    pre = alpha * x_ref[...] + jnp.dot(h_ref[...], w_ref[...], preferred_element_type=F32)
    for jj in range(nj):
        @pl.when(j == jj)
        def _(jj=jj):
            o_ref[:, jj * tn:(jj + 1) * tn] = pre

    @pl.when(j == nj - 1)
    def _():
        xn = _layernorm_rows(o_ref[...], g_ref[...], b_ref[...])
        o_ref[...] = xn
        half = xn.shape[1] // 2
        p_ref[...] = _pack_bf16_pair(xn[:, :half], xn[:, half:])


def _outproj_ln(heads, w, x, g, b, alpha, tm=512, tn=512):
    s, k = heads.shape
    n = w.shape[1]
    tm, tn = min(tm, s), min(tn, n)
    nj = n // tn
    return pl.pallas_call(
        functools.partial(_outproj_kernel, alpha=alpha, tn=tn, nj=nj),
        grid=(s // tm, nj),
        in_specs=[
            pl.BlockSpec((tm, k), lambda i, j: (i, 0)),
            pl.BlockSpec((k, tn), lambda i, j: (0, j)),
            pl.BlockSpec((tm, tn), lambda i, j: (i, j)),
            pl.BlockSpec((1, n), lambda i, j: (0, 0)),
            pl.BlockSpec((1, n), lambda i, j: (0, 0)),
        ],
        out_specs=[pl.BlockSpec((tm, n), lambda i, j: (i, 0)), pl.BlockSpec((tm, n // 2), lambda i, j: (i, 0))],
        out_shape=[jax.ShapeDtypeStruct((s, n), F32), jax.ShapeDtypeStruct((s, n // 2), U32)],
        compiler_params=_cparams(("arbitrary", "arbitrary")),
        name="outproj_ln",
    )(heads, w, x, g, b)


def _first_argmax(v, iota, n):
    m = jnp.max(v, axis=0, keepdims=True)
    first = jnp.min(jnp.where(v == m, iota, n), axis=0, keepdims=True)
    return m, first


def _router_kernel(x_ref, wh_ref, wl_ref, bias_ref, tri_ref, e_ref, g_ref, r_ref, cnt_ref, carry_scr):
    @pl.when(pl.program_id(0) == 0)
    def _():
        carry_scr[...] = jnp.zeros(carry_scr.shape, F32)

    x = x_ref[...]
    xh = x.astype(BF16)
    xl = (x - xh.astype(F32)).astype(BF16)
    dn = (((1,), (1,)), ((), ()))
    wh, wl = wh_ref[...], wl_ref[...]
    logits = (lax.dot_general(wh, xh, dn, preferred_element_type=F32)
              + lax.dot_general(wh, xl, dn, preferred_element_type=F32)
              + lax.dot_general(wl, xh, dn, preferred_element_type=F32))
    scores = _sigmoid(logits)
    sel = scores + bias_ref[...]
    t = sel.shape[1]
    per = N_EXPERTS // N_GROUPS
    ninf = -jnp.inf
    iota_g = lax.broadcasted_iota(I32, (per, t), 0)
    gs_rows = []
    for gi in range(N_GROUPS):
        blk = sel[gi * per:(gi + 1) * per, :]
        m1, f1 = _first_argmax(blk, iota_g, per)
        m2 = jnp.max(jnp.where(iota_g == f1, ninf, blk), axis=0, keepdims=True)
        gs_rows.append(m1 + m2)
    gs = jnp.concatenate(gs_rows, axis=0)
    iota_ng = lax.broadcasted_iota(I32, (N_GROUPS, t), 0)
    gmask = jnp.zeros((N_GROUPS, t), I32)
    for _ in range(TOPK_GROUPS):
        _, f = _first_argmax(gs, iota_ng, N_GROUPS)
        pick = iota_ng == f
        gmask = jnp.where(pick, 1, gmask)
        gs = jnp.where(pick, ninf, gs)
    selm = jnp.concatenate(
        [jnp.where(gmask[gi:gi + 1, :] > 0, sel[gi * per:(gi + 1) * per, :], ninf) for gi in range(N_GROUPS)], axis=0)
    iota_e = lax.broadcasted_iota(I32, (N_EXPERTS, t), 0)
    onehot = jnp.zeros((N_EXPERTS, t), F32)
    e_rows, g_rows = [], []
    for _ in range(TOP_K):
        _, f = _first_argmax(selm, iota_e, N_EXPERTS)
        pick = iota_e == f
        e_rows.append(f)
        g_rows.append(jnp.sum(jnp.where(pick, scores, 0.0), axis=0, keepdims=True))
        onehot = jnp.where(pick, 1.0, onehot)
        selm = jnp.where(pick, ninf, selm)
    gate = jnp.concatenate(g_rows, axis=0)
    gate = gate / jnp.sum(gate, axis=0, keepdims=True) * ROUTED_SCALE
    cum = jnp.dot(onehot.astype(BF16), tri_ref[...], preferred_element_type=F32)
    rank_e = cum - onehot + carry_scr[:, 0:1]
    r_rows = [jnp.sum(jnp.where(iota_e == f, rank_e, 0.0), axis=0, keepdims=True) for f in e_rows]
    carry = carry_scr[...] + cum[:, t - 1:t]
    carry_scr[...] = carry
    e_ref[...] = jnp.concatenate(e_rows, axis=0)
    g_ref[...] = gate
    r_ref[...] = jnp.concatenate(r_rows, axis=0).astype(I32)
    cnt_ref[...] = carry


def _router(x, wh, wl, bias, tm=512):
    t, d = x.shape
    tm = min(tm, t)
    tri = (jnp.arange(tm)[:, None] <= jnp.arange(tm)[None, :]).astype(BF16)
    return pl.pallas_call(
        _router_kernel,
        grid=(t // tm,),
        in_specs=[
            pl.BlockSpec((tm, d), lambda i: (i, 0)),
            pl.BlockSpec((N_EXPERTS, d), lambda i: (0, 0)),
            pl.BlockSpec((N_EXPERTS, d), lambda i: (0, 0)),
            pl.BlockSpec((N_EXPERTS, 1), lambda i: (0, 0)),
            pl.BlockSpec((tm, tm), lambda i: (0, 0)),
        ],
        out_specs=[
            pl.BlockSpec((TOP_K, tm), lambda i: (0, i)),
            pl.BlockSpec((TOP_K, tm), lambda i: (0, i)),
            pl.BlockSpec((TOP_K, tm), lambda i: (0, i)),
            pl.BlockSpec((N_EXPERTS, LANES), lambda i: (0, 0)),
        ],
        out_shape=[
            jax.ShapeDtypeStruct((TOP_K, t), I32),
            jax.ShapeDtypeStruct((TOP_K, t), F32),
            jax.ShapeDtypeStruct((TOP_K, t), I32),
            jax.ShapeDtypeStruct((N_EXPERTS, LANES), F32),
        ],
        scratch_shapes=[pltpu.VMEM((N_EXPERTS, LANES), F32)],
        compiler_params=_cparams(("arbitrary",)),
        name="router",
    )(x, wh, wl, bias, tri)


def _dispatch_kernel(pos_ref, x_ref, xs_hbm, sem, *, tt):
    def issue(t, carry):
        for k in range(TOP_K):
            pltpu.make_async_copy(x_ref.at[pl.ds(t, 1)], xs_hbm.at[pl.ds(pos_ref[k, t], 1)], sem).start(
                priority=k % 2)
        return carry

    lax.fori_loop(0, tt, issue, 0, unroll=4)
    for k in range(TOP_K):
        pltpu.make_async_copy(x_ref, xs_hbm.at[pl.ds(0, tt)], sem).wait()


def _dispatch(pos, xp, tt=512):
    t, w = xp.shape
    tt = min(tt, t)
    return pl.pallas_call(
        functools.partial(_dispatch_kernel, tt=tt),
        grid=(t // tt,),
        in_specs=[
            pl.BlockSpec((TOP_K, tt), lambda i: (0, i), memory_space=pltpu.SMEM),
            pl.BlockSpec((tt, w), lambda i: (i, 0)),
        ],
        out_specs=pl.BlockSpec(memory_space=pl.ANY),
        out_shape=jax.ShapeDtypeStruct((t * TOP_K, w), U32),
        scratch_shapes=[pltpu.SemaphoreType.DMA(())],
        compiler_params=_cparams(("arbitrary",)),
        name="dispatch",
    )(pos, xp)


def _experts_kernel(vblk_ref, vexp_ref, vlo_ref, vhi_ref, vfirst_ref, vnew_ref, xs_ref, wg_ref, wu_ref, wdn_ref,
                    ys_ref, wgu_scr, wd_scr, *, bm):
    v = pl.program_id(0)
    lo, hi = vlo_ref[v], vhi_ref[v]

    @pl.when(vnew_ref[v] == 1)
    def _():
        wgu_scr[:, :D_FF] = wg_ref[...].astype(BF16)
        wgu_scr[:, D_FF:] = wu_ref[...].astype(BF16)
        wd_scr[...] = wdn_ref[...].astype(BF16)

    @pl.when(vfirst_ref[v] == 1)
    def _():
        ys_ref[...] = jnp.zeros(ys_ref.shape, U32)

    @pl.when(hi > lo)
    def _():
        half = xs_ref.shape[1]
        gu = None
        for c in range(half // MOE_CHUNK):
            cs_ = slice(c * MOE_CHUNK, (c + 1) * MOE_CHUNK)
            cs_hi = slice(half + c * MOE_CHUNK, half + (c + 1) * MOE_CHUNK)
            xa, xb = _unpack_bf16_pair(xs_ref[:, cs_])
            part = (jnp.dot(xa.astype(BF16), wgu_scr[cs_, :], preferred_element_type=F32)
                    + jnp.dot(xb.astype(BF16), wgu_scr[cs_hi, :], preferred_element_type=F32))
            gu = part if gu is None else gu + part
        hmid = ((gu[:, :D_FF] * _sigmoid(gu[:, :D_FF])) * gu[:, D_FF:]).astype(BF16)
        rows = vblk_ref[v] * bm + lax.broadcasted_iota(I32, (bm, MOE_CHUNK), 0)
        mine = (rows >= lo) & (rows < hi)
        for c in range(half // MOE_CHUNK):
            cs_ = slice(c * MOE_CHUNK, (c + 1) * MOE_CHUNK)
            cs_hi = slice(half + c * MOE_CHUNK, half + (c + 1) * MOE_CHUNK)
            y_lo = jnp.dot(hmid, wd_scr[:, cs_], preferred_element_type=F32)
            y_hi = jnp.dot(hmid, wd_scr[:, cs_hi], preferred_element_type=F32)
            ys_ref[:, cs_] = jnp.where(mine, _pack_bf16_pair(y_lo, y_hi), ys_ref[:, cs_])


def _experts(sched, xs, w_gate, w_up, w_down, layer, bm):
    r, w = xs.shape
    d = w_down.shape[3]
    nv = sched[0].shape[0]
    grid_spec = pltpu.PrefetchScalarGridSpec(
        num_scalar_prefetch=6,
        grid=(nv,),
        in_specs=[
            pl.BlockSpec((bm, w), lambda v, vb, ve, lo, hi, fi, nw: (vb[v], 0)),
            pl.BlockSpec((None, None, d, D_FF), lambda v, vb, ve, lo, hi, fi, nw: (layer, ve[v], 0, 0)),
            pl.BlockSpec((None, None, d, D_FF), lambda v, vb, ve, lo, hi, fi, nw: (layer, ve[v], 0, 0)),
            pl.BlockSpec((None, None, D_FF, d), lambda v, vb, ve, lo, hi, fi, nw: (layer, ve[v], 0, 0)),
        ],
        out_specs=pl.BlockSpec((bm, w), lambda v, vb, ve, lo, hi, fi, nw: (vb[v], 0)),
        scratch_shapes=[pltpu.VMEM((d, 2 * D_FF), BF16), pltpu.VMEM((D_FF, d), BF16)],
    )
    return pl.pallas_call(
        functools.partial(_experts_kernel, bm=bm),
        grid_spec=grid_spec,
        out_shape=jax.ShapeDtypeStruct((r, w), U32),
        compiler_params=_cparams(("arbitrary",)),
        name="experts",
    )(*sched, xs, w_gate, w_up, w_down)


def _expert_schedule(counts, n_rows, bm):
    counts = counts.astype(I32)
    ends = jnp.cumsum(counts)
    starts = ends - counts
    nb = n_rows // bm
    nv = nb + N_EXPERTS - 1
    first_blk = starts // bm
    last_blk = jnp.maximum(ends - 1, 0) // bm
    nvis = jnp.where(counts > 0, last_blk - first_blk + 1, 0)
    vis_end = jnp.cumsum(nvis)
    vis_start = vis_end - nvis
    total = vis_end[-1]
    v = jnp.arange(nv, dtype=I32)
    e = jnp.minimum(jnp.sum((vis_end[None, :] <= v[:, None]).astype(I32), axis=1), N_EXPERTS - 1)
    real = v < total
    e = jnp.where(real, e, jnp.max(jnp.where(real, e, 0)))
    is_e = e[:, None] == jnp.arange(N_EXPERTS, dtype=I32)[None, :]

    def per_visit(table):
        return jnp.sum(jnp.where(is_e, table[None, :], 0), axis=1)

    blk = per_visit(first_blk) + (v - per_visit(vis_start))
    blk = jnp.where(real, blk, jnp.max(jnp.where(real, blk, 0))).astype(I32)
    lo = jnp.where(real, per_visit(starts), 0).astype(I32)
    hi = jnp.where(real, per_visit(ends), 0).astype(I32)
    prev_blk = jnp.concatenate([jnp.full((1,), -1, I32), blk[:-1]])
    first = (real & (blk != prev_blk)).astype(I32)
    prev_e = jnp.concatenate([jnp.full((1,), -1, I32), e[:-1]])
    new_e = (e != prev_e).astype(I32)
    return starts, (blk, e, lo, hi, first, new_e)


def _combine_kernel(pos_ref, posn_ref, ys_hbm, gate_ref, x_ref, wsgu_ref, wsd_ref, g_ref, b_ref, o_ref, ob_ref,
                    buf, sem, *, tc, alpha, n_steps):
    i = pl.program_id(0)
    slot = i % 2

    def gather(p_ref, dst_slot):
        def issue(t, carry):
            for k in range(TOP_K):
                pltpu.make_async_copy(ys_hbm.at[pl.ds(p_ref[k, t], 1)], buf.at[dst_slot, k, pl.ds(t, 1)],
                                      sem.at[dst_slot]).start(priority=k % 2)
            return carry

        lax.fori_loop(0, tc, issue, 0, unroll=4)

    @pl.when(i == 0)
    def _():
        gather(pos_ref, 0)

    @pl.when(i + 1 < n_steps)
    def _():
        gather(posn_ref, 1 - slot)

    x = x_ref[...]
    xb = x.astype(BF16)
    gu = jnp.dot(xb, wsgu_ref[...], preferred_element_type=F32)
    hmid = (gu[:, :D_FF] * _sigmoid(gu[:, :D_FF])) * gu[:, D_FF:]
    shared = jnp.dot(hmid.astype(BF16), wsd_ref[...], preferred_element_type=F32)
    for k in range(TOP_K):
        pltpu.make_async_copy(ys_hbm.at[pl.ds(0, tc)], buf.at[slot, k], sem.at[slot]).wait()
    d = x.shape[1]
    half = d // 2
    gate = gate_ref[...]
    gks = [jnp.broadcast_to(gate[:, k:k + 1], (tc, LANES)) for k in range(TOP_K)]
    rsum = jnp.zeros((tc, LANES), F32)
    for c in range(half // LANES):
        sl = slice(c * LANES, (c + 1) * LANES)
        sh = slice(half + c * LANES, half + (c + 1) * LANES)
        a_lo, a_hi = shared[:, sl], shared[:, sh]
        for k in range(TOP_K):
            lo, hi = _unpack_bf16_pair(buf[slot, k, :, sl])
            a_lo = a_lo + gks[k] * lo
            a_hi = a_hi + gks[k] * hi
        p_lo = alpha * x_ref[:, sl] + a_lo
        p_hi = alpha * x_ref[:, sh] + a_hi
        o_ref[:, sl] = p_lo
        o_ref[:, sh] = p_hi
        rsum = rsum + (p_lo + p_hi)
    mu = jnp.broadcast_to(jnp.sum(rsum, -1, keepdims=True) * (1.0 / d), (tc, LANES))
    rsq = jnp.zeros((tc, LANES), F32)
    for c in range(d // LANES):
        xc = o_ref[:, c * LANES:(c + 1) * LANES] - mu
        rsq = rsq + xc * xc
    inv = jnp.broadcast_to(lax.rsqrt(jnp.sum(rsq, -1, keepdims=True) * (1.0 / d) + LN_EPS), (tc, LANES))
    for c in range(d // LANES):
        sl = slice(c * LANES, (c + 1) * LANES)
        xn = (o_ref[:, sl] - mu) * inv * g_ref[:, sl] + b_ref[:, sl]
        o_ref[:, sl] = xn
        ob_ref[:, sl] = xn.astype(BF16)


def _combine_ln(pos, ys, gate_tok, x, wsgu, wsd, g, b, alpha, tc=128):
    t, d = x.shape
    tc = min(tc, t)
    w = ys.shape[1]
    n_steps = t // tc
    return pl.pallas_call(
        functools.partial(_combine_kernel, tc=tc, alpha=alpha, n_steps=n_steps),
        grid=(n_steps,),
        in_specs=[
            pl.BlockSpec((TOP_K, tc), lambda i: (0, i), memory_space=pltpu.SMEM),
            pl.BlockSpec((TOP_K, tc), lambda i: (0, jnp.minimum(i + 1, n_steps - 1)), memory_space=pltpu.SMEM),
            pl.BlockSpec(memory_space=pl.ANY),
            pl.BlockSpec((tc, TOP_K), lambda i: (i, 0)),
            pl.BlockSpec((tc, d), lambda i: (i, 0)),
            pl.BlockSpec((d, 2 * D_FF), lambda i: (0, 0)),
            pl.BlockSpec((D_FF, d), lambda i: (0, 0)),
            pl.BlockSpec((1, d), lambda i: (0, 0)),
            pl.BlockSpec((1, d), lambda i: (0, 0)),
        ],
        out_specs=[pl.BlockSpec((tc, d), lambda i: (i, 0)), pl.BlockSpec((tc, d), lambda i: (i, 0))],
        out_shape=[jax.ShapeDtypeStruct((t, d), F32), jax.ShapeDtypeStruct((t, d), BF16)],
        scratch_shapes=[pltpu.VMEM((2, TOP_K, tc, w), U32), pltpu.SemaphoreType.DMA((2,))],
        compiler_params=_cparams(("arbitrary",)),
        name="combine_ln",
    )(pos, pos, ys, gate_tok, x, wsgu, wsd, g, b)


def _swap_rope_cols(w):
    half = w.shape[-1] // 2
    return jnp.concatenate([-w[..., half:], w[..., :half]], axis=-1)


def _prep_w_in(w_in):
    off_kr = Q_RANK + KV_RANK
    w_kr = w_in[:, off_kr:off_kr + ROPE]
    return jnp.concatenate(
        [w_in[:, :off_kr], w_kr, _swap_rope_cols(w_kr), w_in[:, off_kr + ROPE:]], axis=1).astype(BF16)


def _prep_w_uq(w_uq):
    w = w_uq.reshape(Q_RANK, N_HEADS, NOPE + ROPE)
    w_r = w[..., NOPE:]
    return jnp.concatenate([w[..., :NOPE], w_r, _swap_rope_cols(w_r)], axis=-1).reshape(
        Q_RANK, N_HEADS * HEAD_W).astype(BF16)


def kernel(x, positions, w_in, q_norm_g, kv_norm_g, w_uq, w_ukv, ssm_lambda_re, ssm_lambda_im, ssm_log_dt,
           ssm_b_re, ssm_b_im, ssm_c_re, ssm_c_im, ssm_d, w_glu, b_glu, attn_out_norm_g, ssm_out_norm_g,
           w_out, ln1_g, ln1_b, w_router, router_bias, w_gate, w_up, w_down, ws_gate, ws_up, ws_down,
           ln2_g, ln2_b):
    batch, seq, d_model = x.shape
    assert batch == 1
    depth = w_in.shape[0]
    alpha = float((2 * depth) ** 0.25)
    u_col_block = (Q_RANK + KV_RANK + LANES) // LANES

    inv_freq = ROPE_THETA ** (-(jnp.arange(0, ROPE, 2, dtype=F32) / ROPE))
    ang = positions.astype(F32)[0][:, None] * inv_freq
    cos, sin = jnp.cos(ang), jnp.sin(ang)
    cs = jnp.concatenate([cos, cos, sin, sin], axis=1)

    s5_tables = jax.vmap(_s5_params)(ssm_lambda_re, ssm_lambda_im, ssm_log_dt, ssm_b_re, ssm_b_im,
                                     ssm_c_re, ssm_c_im, ssm_d)
    xf = x[0]
    xb = xf
    for l in range(depth):
        proj = _matmul(xb, _prep_w_in(w_in[l]), F32, tm=512, tn=1152)
        w_kv = w_ukv[l].reshape(KV_RANK, N_HEADS, NOPE + V_DIM)
        wkt = w_kv[..., :NOPE].reshape(KV_RANK, N_HEADS * NOPE).T.astype(BF16)
        wv = w_kv[..., NOPE:].reshape(KV_RANK, N_HEADS * V_DIM).astype(BF16)
        q, kt, vp = _mla_prep(proj, cs, q_norm_g[l][None], kv_norm_g[l][None],
                              _prep_w_uq(w_uq[l]), wkt, wv, min(MLA_TM, seq))
        attn = _flash(q, kt, vp)
        y = _s5(proj, u_col_block, s5_tables, l)
        heads = _heads(attn, y, attn_out_norm_g[l][None], ssm_out_norm_g[l][None],
                       w_glu[l].astype(BF16), b_glu[l][None])
        x1, x1p = _outproj_ln(heads, w_out[l].astype(BF16), xf, ln1_g[l][None], ln1_b[l][None], alpha)
        wr_t = w_router[l].T
        wr_h = wr_t.astype(BF16)
        wr_l = (wr_t - wr_h.astype(F32)).astype(BF16)
        top_e, gate, rank, cnt = _router(x1, wr_h, wr_l, router_bias[l][:, None])
        starts, sched = _expert_schedule(cnt[:, 0], seq * TOP_K, min(MOE_BM, seq * TOP_K))
        e_ids = jnp.arange(N_EXPERTS, dtype=I32)[:, None, None]
        pos = rank + jnp.sum(jnp.where(top_e[None] == e_ids, starts[:, None, None], 0), axis=0)
        xs = _dispatch(pos, x1p)
        ys = _experts(sched, xs, w_gate, w_up, w_down, l, min(MOE_BM, seq * TOP_K))
        wsgu = jnp.concatenate([ws_gate[l], ws_up[l]], axis=-1).astype(BF16)
        xf, xb = _combine_ln(pos, ys, gate.T, x1, wsgu, ws_down[l].astype(BF16),
                             ln2_g[l][None], ln2_b[l][None], alpha)
    return xf[None]
```
